```python
import math
import jax
import jax.numpy as jnp
from jax import lax
import numpy as np

D_MODEL = 1024
BATCH = 2
SEQ = 8192
DEPTH = 2

HEAD_DIM = 64
BRANCH_WIDTH = 512
N_BRANCH = 4
SSM_HEADS = 8
SSM_HEAD_DIM = 64
SSM_INNER = SSM_HEADS * SSM_HEAD_DIM
SSM_GROUPS = 2
SSM_STATE = 128
SSM_CONV = 4
SSM_CHUNK = 256
SSM_CONV_CH = SSM_INNER + 2 * SSM_GROUPS * SSM_STATE
FOX_HEADS = 8
MOBA_HEADS = 8
MOBA_BLOCK = 256
MOBA_TOPK = 3
MOBA_Q_BLOCK = 64
DSA_HEADS = 8
IDX_HEADS = 4
IDX_DIM = 64
DSA_TOPK = 256
Q_BLOCK = 128
ROPE_THETA = 10000.0
FFN_HIDDEN = -(-8 * D_MODEL // (3 * 256)) * 256
EPS = 1e-6

IN_SPLITS = (
    SSM_INNER, SSM_INNER, SSM_GROUPS * SSM_STATE, SSM_GROUPS * SSM_STATE, SSM_HEADS,
    FOX_HEADS * HEAD_DIM, FOX_HEADS * HEAD_DIM, FOX_HEADS * HEAD_DIM, FOX_HEADS,
    MOBA_HEADS * HEAD_DIM, MOBA_HEADS * HEAD_DIM, MOBA_HEADS * HEAD_DIM,
    DSA_HEADS * HEAD_DIM, DSA_HEADS * HEAD_DIM, DSA_HEADS * HEAD_DIM,
    IDX_HEADS * IDX_DIM, IDX_DIM, IDX_HEADS,
)
D_IN = sum(IN_SPLITS)

kernel_name = 'hybrid_gated_ssd_fox_moba_dsa'


def rms_norm(x, g):
    x32 = x.astype(jnp.float32)
    y = x32 * lax.rsqrt(jnp.mean(x32 * x32, axis=-1, keepdims=True) + EPS)
    return (y * g.astype(jnp.float32)).astype(x.dtype)


def rope_tables(seq):
    pos = jnp.arange(seq, dtype=jnp.float32)
    inv = ROPE_THETA ** (-jnp.arange(0, HEAD_DIM, 2, dtype=jnp.float32) / HEAD_DIM)
    ang = pos[:, None] * inv[None, :]
    return jnp.cos(ang), jnp.sin(ang)


def apply_rope(t, cos, sin):
    t1, t2 = jnp.split(t.astype(jnp.float32), 2, axis=-1)
    c = cos[None, :, None, :]
    s = sin[None, :, None, :]
    return jnp.concatenate([t1 * c - t2 * s, t2 * c + t1 * s], axis=-1).astype(t.dtype)


def split_heads(t, n_heads):
    return t.reshape(t.shape[0], t.shape[1], n_heads, -1)


def causal_dwconv(u, w, b):
    width = w.shape[0]
    up = jnp.pad(u, ((0, 0), (width - 1, 0), (0, 0)))
    y = lax.conv_general_dilated(up, w[:, None, :], window_strides=(1,), padding='VALID',
                                 dimension_numbers=('NWC', 'WIO', 'NWC'),
                                 feature_group_count=u.shape[-1])
    return y + b


def ssd_chunked(x, dt, a, bmat, cmat):
    bsz, seq, n_heads, hp = x.shape
    pad = (-seq) % SSM_CHUNK
    padl = lambda t: jnp.pad(t, ((0, 0), (0, pad)) + ((0, 0),) * (t.ndim - 2))
    n_chunk = (seq + pad) // SSM_CHUNK
    chunk = lambda t: padl(t).reshape((bsz, n_chunk, SSM_CHUNK) + t.shape[2:])
    xc, dtc, bc, cc = chunk(x), chunk(dt), chunk(bmat), chunk(cmat)
    a_cs = jnp.cumsum((dtc * a).transpose(0, 3, 1, 2), axis=-1)
    xdt = xc * dtc[..., None]
    causal = jnp.tril(jnp.ones((SSM_CHUNK, SSM_CHUNK), dtype=bool))
    seg = a_cs[..., :, None] - a_cs[..., None, :]
    decay = jnp.exp(jnp.where(causal, seg, -jnp.inf))
    scores = jnp.einsum('bclhn,bcshn->bhcls', cc, bc) * decay
    y_diag = jnp.einsum('bhcls,bcshp->bclhp', scores, xdt)
    decay_states = jnp.exp(a_cs[..., -1:] - a_cs)
    states = jnp.einsum('bclhn,bhcl,bclhp->bchpn', bc, decay_states, xdt)
    chunk_decay = jnp.exp(a_cs[..., -1])

    def step(h, inp):
        s_c, d_c = inp
        return h * d_c[..., None, None] + s_c, h

    h0 = jnp.zeros((bsz, n_heads, hp, bmat.shape[-1]), jnp.float32)
    _, prev = lax.scan(step, h0, (states.transpose(1, 0, 2, 3, 4), chunk_decay.transpose(2, 0, 1)))
    prev = prev.transpose(1, 0, 2, 3, 4)
    y_off = jnp.einsum('bclhn,bchpn,bhcl->bclhp', cc, prev, jnp.exp(a_cs))
    y = (y_diag + y_off).reshape(bsz, n_chunk * SSM_CHUNK, n_heads, hp)
    return y[:, :seq]


def mamba2_branch(z, xs, bs, cs, dt_raw, conv_w, conv_b, dt_bias, a_log, d_skip, norm_w):
    bsz, seq, _ = z.shape
    xbc = jax.nn.silu(causal_dwconv(jnp.concatenate([xs, bs, cs], axis=-1), conv_w, conv_b))
    xs, bs, cs = jnp.split(xbc, [SSM_INNER, SSM_INNER + SSM_GROUPS * SSM_STATE], axis=-1)
    rep = SSM_HEADS // SSM_GROUPS
    xh = xs.reshape(bsz, seq, SSM_HEADS, SSM_HEAD_DIM).astype(jnp.float32)
    bh = jnp.repeat(bs.reshape(bsz, seq, SSM_GROUPS, SSM_STATE), rep, axis=2).astype(jnp.float32)
    ch = jnp.repeat(cs.reshape(bsz, seq, SSM_GROUPS, SSM_STATE), rep, axis=2).astype(jnp.float32)
    dt = jax.nn.softplus(dt_raw.astype(jnp.float32) + dt_bias.astype(jnp.float32))
    a = -jnp.exp(a_log.astype(jnp.float32))
    y = ssd_chunked(xh, dt, a, bh, ch) + d_skip.astype(jnp.float32)[:, None] * xh
    y = y.reshape(bsz, seq, SSM_INNER) * jax.nn.silu(z.astype(jnp.float32))
    yg = y.reshape(bsz, seq, SSM_GROUPS, -1)
    yg = yg * lax.rsqrt(jnp.mean(yg * yg, axis=-1, keepdims=True) + EPS)
    return (yg.reshape(bsz, seq, SSM_INNER) * norm_w.astype(jnp.float32)).astype(z.dtype)


def fox_branch(q, k, v, f_logit, f_bias):
    bsz, seq, n_heads, hd = q.shape
    log_f = jax.nn.log_sigmoid(f_logit.astype(jnp.float32) + f_bias.astype(jnp.float32))
    cf = jnp.cumsum(log_f, axis=1).transpose(0, 2, 1)
    kpos = jnp.arange(seq)
    scale = hd ** -0.5

    def block(i):
        start = i * Q_BLOCK
        qb = lax.dynamic_slice_in_dim(q, start, Q_BLOCK, axis=1)
        cq = lax.dynamic_slice_in_dim(cf, start, Q_BLOCK, axis=2)
        qpos = start + jnp.arange(Q_BLOCK)
        logits = jnp.einsum('bqhd,bkhd->bhqk', qb, k).astype(jnp.float32) * scale
        logits = logits + cq[..., :, None] - cf[..., None, :]
        logits = jnp.where(kpos[None, :] <= qpos[:, None], logits, -jnp.inf)
        p = jax.nn.softmax(logits, axis=-1).astype(v.dtype)
        return jnp.einsum('bhqk,bkhd->bqhd', p, v)

    out = lax.map(block, jnp.arange(seq // Q_BLOCK))
    return out.transpose(1, 0, 2, 3, 4).reshape(bsz, seq, n_heads * hd)


def moba_branch(q, k, v):
    bsz, seq, n_heads, hd = q.shape
    n_blk = -(-seq // MOBA_BLOCK)
    pad = n_blk * MOBA_BLOCK - seq
    to_blocks = lambda t: jnp.pad(t, ((0, 0), (0, pad), (0, 0), (0, 0))).reshape(
        bsz, n_blk, MOBA_BLOCK, n_heads, hd).transpose(0, 3, 1, 2, 4)
    k_blk, v_blk = to_blocks(k), to_blocks(v)
    k_mean = jnp.mean(k_blk.astype(jnp.float32), axis=3)
    n_sel = min(MOBA_TOPK, n_blk)
    n_sel_keys = n_sel * MOBA_BLOCK
    scale = hd ** -0.5
    b_ix = jnp.arange(bsz)[:, None, None, None]
    h_ix = jnp.arange(n_heads)[None, None, :, None]
    blk_ids = jnp.arange(n_blk)

    def block(i):
        start = i * MOBA_Q_BLOCK
        qb = lax.dynamic_slice_in_dim(q, start, MOBA_Q_BLOCK, axis=1)
        qpos = start + jnp.arange(MOBA_Q_BLOCK)
        cur = start // MOBA_BLOCK
        gate = jnp.einsum('bqhd,bhnd->bqhn', qb.astype(jnp.float32), k_mean)
        gate = jnp.where(blk_ids < cur, gate, -jnp.inf)
        _, sel = lax.top_k(gate, n_sel)
        sel_ok = sel < cur
        k_sel = k_blk[b_ix, h_ix, sel]
        v_sel = v_blk[b_ix, h_ix, sel]
        s_sel = jnp.einsum('bqhd,bqhnkd->bqhnk', qb, k_sel).astype(jnp.float32) * scale
        s_sel = jnp.where(sel_ok[..., None], s_sel, -jnp.inf)
        k_own = lax.dynamic_index_in_dim(k_blk, cur, axis=2, keepdims=False)
        v_own = lax.dynamic_index_in_dim(v_blk, cur, axis=2, keepdims=False)
        own_pos = cur * MOBA_BLOCK + jnp.arange(MOBA_BLOCK)
        s_own = jnp.einsum('bqhd,bhkd->bqhk', qb, k_own).astype(jnp.float32) * scale
        s_own = jnp.where(own_pos[None, None, None, :] <= qpos[None, :, None, None], s_own, -jnp.inf)
        logits = jnp.concatenate([s_sel.reshape(bsz, MOBA_Q_BLOCK, n_heads, n_sel_keys), s_own], axis=-1)
        p = jax.nn.softmax(logits, axis=-1).astype(v.dtype)
        p_sel = p[..., :n_sel_keys].reshape(bsz, MOBA_Q_BLOCK, n_heads, n_sel, MOBA_BLOCK)
        p_own = p[..., n_sel_keys:]
        return (jnp.einsum('bqhnk,bqhnkd->bqhd', p_sel, v_sel)
                + jnp.einsum('bqhk,bhkd->bqhd', p_own, v_own))

    out = lax.map(block, jnp.arange(seq // MOBA_Q_BLOCK))
    return out.transpose(1, 0, 2, 3, 4).reshape(bsz, seq, n_heads * hd)


def dsa_branch(q, k, v, qi, ki, wi):
    bsz, seq, n_heads, hd = q.shape
    n_keep = min(DSA_TOPK, seq // 4)
    kpos = jnp.arange(seq)
    scale = hd ** -0.5
    idx_scale = IDX_DIM ** -0.5
    w_scale = IDX_HEADS ** -0.5
    b_ix = jnp.arange(bsz)[:, None, None]

    def block(i):
        start = i * Q_BLOCK
        qb = lax.dynamic_slice_in_dim(q, start, Q_BLOCK, axis=1)
        qib = lax.dynamic_slice_in_dim(qi, start, Q_BLOCK, axis=1)
        wib = lax.dynamic_slice_in_dim(wi, start, Q_BLOCK, axis=1)
        qpos = start + jnp.arange(Q_BLOCK)
        isc = jax.nn.relu(jnp.einsum('bqhd,bkd->bqhk', qib, ki).astype(jnp.float32) * idx_scale)
        isc = jnp.einsum('bqh,bqhk->bqk', wib.astype(jnp.float32) * w_scale, isc)
        isc = jnp.where(kpos[None, None, :] <= qpos[None, :, None], isc, -jnp.inf)
        _, sel = lax.top_k(isc, n_keep)
        ok = sel <= qpos[None, :, None]
        k_sel = k[b_ix, sel]
        v_sel = v[b_ix, sel]
        logits = jnp.einsum('bqhd,bqkhd->bqhk', qb, k_sel).astype(jnp.float32) * scale
        logits = jnp.where(ok[:, :, None, :], logits, -jnp.inf)
        p = jax.nn.softmax(logits, axis=-1).astype(v.dtype)
        return jnp.einsum('bqhk,bqkhd->bqhd', p, v_sel)

    out = lax.map(block, jnp.arange(seq // Q_BLOCK))
    return out.transpose(1, 0, 2, 3, 4).reshape(bsz, seq, n_heads * hd)


def hybrid_mixer(h, w_in, conv_w, conv_b, dt_bias, a_log, d_skip, ssm_norm, fox_fbias,
                 w_gate, b_gate, w_branch, w_out, cos, sin):
    bsz, seq, _ = h.shape
    u = h @ w_in
    (z, xs, bs, cs, dt_raw, fq, fk, fv, ff, mq, mk, mv, dq, dk, dv, iq, ik, iw) = jnp.split(
        u, np.cumsum(IN_SPLITS)[:-1].tolist(), axis=-1)
    y_ssm = mamba2_branch(z, xs, bs, cs, dt_raw, conv_w, conv_b, dt_bias, a_log, d_skip, ssm_norm)
    y_fox = fox_branch(split_heads(fq, FOX_HEADS), split_heads(fk, FOX_HEADS),
                       split_heads(fv, FOX_HEADS), ff, fox_fbias)
    y_moba = moba_branch(apply_rope(split_heads(mq, MOBA_HEADS), cos, sin),
                         apply_rope(split_heads(mk, MOBA_HEADS), cos, sin),
                         split_heads(mv, MOBA_HEADS))
    y_dsa = dsa_branch(apply_rope(split_heads(dq, DSA_HEADS), cos, sin),
                       apply_rope(split_heads(dk, DSA_HEADS), cos, sin),
                       split_heads(dv, DSA_HEADS),
                       apply_rope(split_heads(iq, IDX_HEADS), cos, sin),
                       apply_rope(ik[:, :, None, :], cos, sin)[:, :, 0, :],
                       iw)
    ys = jnp.stack([y_ssm, y_fox, y_moba, y_dsa], axis=2)
    proj = jnp.einsum('bsnc,ncd->bsnd', ys, w_branch)
    gates = jax.nn.sigmoid(h @ w_gate + b_gate).reshape(bsz, seq, N_BRANCH, D_MODEL)
    merged = jnp.sum(gates * proj, axis=2)
    return merged @ w_out


def swiglu(h, w_in, w_out):
    g, up = jnp.split(h @ w_in, 2, axis=-1)
    return (jax.nn.silu(g) * up) @ w_out


def setup_inputs(seed: int = 0) -> dict:
    key = jax.random.key(seed)
    ks = jax.random.split(key, 19)
    f32 = jnp.float32
    nrm = lambda k, shape, scale: jax.random.normal(k, shape, f32) * scale
    gain = lambda k, shape: 1.0 + 0.05 * jax.random.normal(k, shape, f32)
    dt0 = jnp.exp(jax.random.uniform(ks[5], (DEPTH, SSM_HEADS), f32, math.log(1e-3), math.log(1e-1)))
    return {
        'x': nrm(ks[0], (BATCH, SEQ, D_MODEL), 1.0),
        'norm_mix_pre': gain(ks[1], (DEPTH, D_MODEL)),
        'w_in': nrm(ks[2], (DEPTH, D_MODEL, D_IN), D_MODEL ** -0.5),
        'conv_w': nrm(ks[3], (DEPTH, SSM_CONV, SSM_CONV_CH), SSM_CONV ** -0.5),
        'conv_b': nrm(ks[4], (DEPTH, SSM_CONV_CH), 0.02),
        'dt_bias': dt0 + jnp.log(-jnp.expm1(-dt0)),
        'a_log': jnp.log(jax.random.uniform(ks[6], (DEPTH, SSM_HEADS), f32, 1.0, 16.0)),
        'd_skip': 1.0 + 0.1 * jax.random.normal(ks[7], (DEPTH, SSM_HEADS), f32),
        'ssm_norm': gain(ks[8], (DEPTH, SSM_INNER)),
        'fox_fbias': 3.0 + 0.5 * jax.random.normal(ks[9], (DEPTH, FOX_HEADS), f32),
        'w_gate': nrm(ks[10], (DEPTH, D_MODEL, N_BRANCH * D_MODEL), D_MODEL ** -0.5),
        'b_gate': nrm(ks[11], (DEPTH, N_BRANCH * D_MODEL), 0.02),
        'w_branch': nrm(ks[12], (DEPTH, N_BRANCH, BRANCH_WIDTH, D_MODEL), BRANCH_WIDTH ** -0.5),
        'w_out': nrm(ks[13], (DEPTH, D_MODEL, D_MODEL), D_MODEL ** -0.5),
        'norm_mix_post': gain(ks[14], (DEPTH, D_MODEL)),
        'norm_ffn_pre': gain(ks[15], (DEPTH, D_MODEL)),
        'w_ffn_in': nrm(ks[16], (DEPTH, D_MODEL, 2 * FFN_HIDDEN), D_MODEL ** -0.5),
        'w_ffn_out': nrm(ks[17], (DEPTH, FFN_HIDDEN, D_MODEL), FFN_HIDDEN ** -0.5),
        'norm_ffn_post': gain(ks[18], (DEPTH, D_MODEL)),
    }


def reference(x, norm_mix_pre, w_in, conv_w, conv_b, dt_bias, a_log, d_skip, ssm_norm, fox_fbias,
              w_gate, b_gate, w_branch, w_out, norm_mix_post, norm_ffn_pre, w_ffn_in, w_ffn_out,
              norm_ffn_post):
    cos, sin = rope_tables(x.shape[1])
    for l in range(DEPTH):
        h = rms_norm(x, norm_mix_pre[l])
        mix = hybrid_mixer(h, w_in[l], conv_w[l], conv_b[l], dt_bias[l], a_log[l], d_skip[l],
                           ssm_norm[l], fox_fbias[l], w_gate[l], b_gate[l], w_branch[l], w_out[l],
                           cos, sin)
        x = x + rms_norm(mix, norm_mix_post[l])
        h = rms_norm(x, norm_ffn_pre[l])
        x = x + rms_norm(swiglu(h, w_ffn_in[l], w_ffn_out[l]), norm_ffn_post[l])
    return x
```

```python
import functools
import math

import jax
import jax.numpy as jnp
from jax import lax
from jax.experimental import pallas as pl
from jax.experimental.pallas import tpu as pltpu

F32 = jnp.float32
BF16 = jnp.bfloat16

D_MODEL = 1024
HEAD_DIM = 64
N_HEADS = 8
WIDTH = N_HEADS * HEAD_DIM
N_BRANCH = 4
SSM_GROUPS = 2
SSM_STATE = 128
SSM_CONV = 4
SSM_CHUNK = 256
MOBA_BLOCK = 256
MOBA_TOPK = 3
IDX_HEADS = 4
IDX_DIM = 64
DSA_TOPK = 256
ROPE_THETA = 10000.0
FFN_HIDDEN = -(-8 * D_MODEL // (3 * 256)) * 256
EPS = 1e-6

LANES = 128
NEG = -1e30
VMEM_LIMIT = 56 * 1024 * 1024

COL_XBC, COL_Z, COL_DV = 0, 8, 12
COL_FQ, COL_FK, COL_FV, COL_MV = 16, 20, 24, 28
COL_DT, COL_IW = 32, 33
COL_IQ, COL_DQ, COL_DK, COL_MQ, COL_MK, COL_IK = 34, 36, 40, 44, 48, 52
N_COLBLK = 54
ROPE_START = COL_IQ
PROJ_TN = 256
ROW_TILE = 256


def _dot(a, b):
    return jnp.dot(a, b, preferred_element_type=F32)


def _dot_nt(a, b):
    return lax.dot_general(a, b, (((1,), (1,)), ((), ())), preferred_element_type=F32)


def _dot_tn(a, b):
    return lax.dot_general(a, b, (((0,), (0,)), ((), ())), preferred_element_type=F32)


def _dot_f32(a, b):
    return jnp.dot(a, b, preferred_element_type=F32, precision=lax.Precision.HIGHEST)


def _rms(x, g):
    return x * lax.rsqrt(jnp.mean(x * x, axis=-1, keepdims=True) + EPS) * g


def _silu(x):
    return x * (1.0 / (1.0 + jnp.exp(-x)))


def _softplus(x):
    return jnp.maximum(x, 0.0) + jnp.log(1.0 + jnp.exp(-jnp.abs(x)))


def _params(sem):
    return pltpu.CompilerParams(dimension_semantics=sem, vmem_limit_bytes=VMEM_LIMIT)


def _inproj_kernel(x_ref, g_ref, w_ref, wt_ref, cos_ref, sin_ref, fb_ref, tri_ref,
                   u_ref, t_ref, km_ref, carry_ref, *, tiles_per_seq):
    i = pl.program_id(0)
    hb = _rms(x_ref[...], g_ref[...]).astype(BF16)
    tm = hb.shape[0]
    lane = lax.broadcasted_iota(jnp.int32, (tm, PROJ_TN), 1)
    first_half = (lane % HEAD_DIM) < (HEAD_DIM // 2)
    n_tiles = N_COLBLK * LANES // PROJ_TN
    for j in range(n_tiles):
        c0 = j * PROJ_TN
        acc = _dot(hb, w_ref[:, c0:c0 + PROJ_TN])
        blk = c0 // LANES
        if blk >= ROPE_START:
            partner = jnp.where(first_half,
                                pltpu.roll(acc, PROJ_TN - HEAD_DIM // 2, 1),
                                pltpu.roll(acc, HEAD_DIM // 2, 1))
            cos = jnp.concatenate([cos_ref[...]] * (PROJ_TN // LANES), axis=1)
            sin = jnp.concatenate([sin_ref[...]] * (PROJ_TN // LANES), axis=1)
            acc = acc * cos + partner * sin
            if COL_MK <= blk < COL_MK + WIDTH // LANES:
                o = (blk - COL_MK) * LANES
                km_ref[0, :, o:o + PROJ_TN] = jnp.mean(acc, axis=0, keepdims=True)
        u_ref[:, c0:c0 + PROJ_TN] = acc

    t = _dot_nt(wt_ref[...], hb)

    @pl.when(i % tiles_per_seq == 0)
    def _():
        carry_ref[...] = jnp.zeros_like(carry_ref)

    logf = -_softplus(-(t[N_HEADS:] + fb_ref[...]))
    cf = _dot_f32(logf, tri_ref[...]) + carry_ref[...]
    carry_ref[...] = cf[:, tm - 1:tm]
    t_ref[0:N_HEADS, :] = t[:N_HEADS]
    t_ref[N_HEADS:, :] = cf


def _rearranged_w_in(w_in):
    o = 0
    parts = {}
    for name, wdt in (("z", 512), ("xs", 512), ("bs", 256), ("cs", 256), ("dt", 8),
                      ("fq", 512), ("fk", 512), ("fv", 512), ("ff", 8),
                      ("mq", 512), ("mk", 512), ("mv", 512),
                      ("dq", 512), ("dk", 512), ("dv", 512),
                      ("iq", 256), ("ik", 64), ("iw", 4)):
        parts[name] = w_in[:, o:o + wdt]
        o += wdt
    d = w_in.shape[0]
    pad = lambda n: jnp.zeros((d, n), w_in.dtype)
    qs = HEAD_DIM ** -0.5
    cols = [parts["xs"], parts["bs"], parts["cs"], parts["z"], parts["dv"],
            parts["fq"] * qs, parts["fk"], parts["fv"], parts["mv"],
            parts["dt"], pad(LANES - 8),
            parts["iw"] * (IDX_HEADS ** -0.5), pad(LANES - 4),
            parts["iq"] * (IDX_DIM ** -0.5), parts["dq"] * qs, parts["dk"],
            parts["mq"] * qs, parts["mk"], parts["ik"], parts["ik"], pad(LANES)]
    w = jnp.concatenate(cols, axis=1).astype(BF16)
    assert w.shape[1] == N_COLBLK * LANES
    wt = jnp.concatenate([parts["dt"], parts["ff"]], axis=1).T.astype(BF16)
    return w, wt


def _rope_tables(seq):
    pos = jnp.arange(seq, dtype=F32)
    inv = ROPE_THETA ** (-jnp.arange(0, HEAD_DIM, 2, dtype=F32) / HEAD_DIM)
    ang = pos[:, None] * inv[None, :]
    c, s = jnp.cos(ang), jnp.sin(ang)
    cos = jnp.concatenate([c, c, c, c], axis=1)
    sin = jnp.concatenate([-s, s, -s, s], axis=1)
    return cos, sin


def _in_projection(x2, g, w, wt, cos, sin, fbias, seq):
    n, d = x2.shape
    tm = ROW_TILE
    tiles_per_seq = seq // tm
    ncol = N_COLBLK * LANES
    tri = jnp.triu(jnp.ones((tm, tm), F32))
    const = lambda i: (0, 0)
    return pl.pallas_call(
        functools.partial(_inproj_kernel, tiles_per_seq=tiles_per_seq),
        grid=(n // tm,),
        in_specs=[
            pl.BlockSpec((tm, d), lambda i: (i, 0)),
            pl.BlockSpec((1, d), const),
            pl.BlockSpec((d, ncol), const),
            pl.BlockSpec((2 * N_HEADS, d), const),
            pl.BlockSpec((tm, LANES), lambda i: (i % tiles_per_seq, 0)),
            pl.BlockSpec((tm, LANES), lambda i: (i % tiles_per_seq, 0)),
            pl.BlockSpec((N_HEADS, 1), const),
            pl.BlockSpec((tm, tm), const),
        ],
        out_specs=[
            pl.BlockSpec((tm, ncol), lambda i: (i, 0)),
            pl.BlockSpec((2 * N_HEADS, tm), lambda i: (0, i)),
            pl.BlockSpec((1, 1, WIDTH), lambda i: (i, 0, 0)),
        ],
        out_shape=[
            jax.ShapeDtypeStruct((n, ncol), F32),
            jax.ShapeDtypeStruct((2 * N_HEADS, n), F32),
            jax.ShapeDtypeStruct((n // tm, 1, WIDTH), F32),
        ],
        scratch_shapes=[pltpu.VMEM((N_HEADS, 1), F32)],
        compiler_params=_params(("arbitrary",)),
        name="in_projection",
    )(x2, g.reshape(1, d), w, wt, cos, sin, fbias.reshape(N_HEADS, 1), tri)


def _ssd_kernel(z_ref, xbc_ref, dtc_ref, t_ref, cw_ref, cb_ref, dtb_row_ref, dtb_col_ref,
                alog_row_ref, alog_col_ref, dskip_ref, nw_ref, tril_ref, triu_ref, expand_ref,
                y_ref, ext_ref, state_ref):
    c = pl.program_id(1)
    q = SSM_CHUNK
    halo = 8
    n_pair = WIDTH // LANES

    @pl.when(c == 0)
    def _():
        ext_ref[0:halo, :] = jnp.zeros((halo, ext_ref.shape[1]), F32)
        state_ref[...] = jnp.zeros_like(state_ref)

    @pl.when(c > 0)
    def _():
        ext_ref[0:halo, :] = ext_ref[q:q + halo, :]

    ext_ref[halo:halo + q, :] = xbc_ref[...]
    conv = cb_ref[...]
    for w in range(SSM_CONV):
        o = halo - (SSM_CONV - 1) + w
        conv = conv + ext_ref[o:o + q, :] * cw_ref[w:w + 1, :]
    xbc = _silu(conv)
    xs = xbc[:, :WIDTH]
    bmat = xbc[:, WIDTH:WIDTH + SSM_GROUPS * SSM_STATE].astype(BF16)
    cmat = xbc[:, WIDTH + SSM_GROUPS * SSM_STATE:].astype(BF16)

    dt_col = _softplus(dtc_ref[...] + dtb_row_ref[...])
    dt_row = _softplus(t_ref[0:N_HEADS, :] + dtb_col_ref[...])
    a_row = -jnp.exp(alog_row_ref[...])
    a_col = -jnp.exp(alog_col_ref[...])
    acs_col = _dot_f32(tril_ref[...], dt_col * a_row)
    acs_row = _dot_f32(dt_row * a_col, triu_ref[...])
    alast_row = acs_col[q - 1:q, :]
    alast_col = acs_row[:, q - 1:q]

    expand = expand_ref[...]
    xdt = xs * _dot_f32(dt_col, expand)
    dstate_x = _dot_f32(jnp.exp(alast_row - acs_col), expand)
    ea_x = _dot_f32(jnp.exp(acs_col), expand)

    li = lax.broadcasted_iota(jnp.int32, (q, q), 0)
    si = lax.broadcasted_iota(jnp.int32, (q, q), 1)
    causal = li >= si
    lane = lax.broadcasted_iota(jnp.int32, (q, LANES), 1)
    sub = lax.broadcasted_iota(jnp.int32, (LANES, 1), 0)

    cb = []
    for g in range(SSM_GROUPS):
        cg = cmat[:, g * SSM_STATE:(g + 1) * SSM_STATE]
        bg = bmat[:, g * SSM_STATE:(g + 1) * SSM_STATE]
        cb.append(_dot_nt(cg, bg))

    for p in range(n_pair):
        g = (2 * p) // (N_HEADS // SSM_GROUPS)
        cg = cmat[:, g * SSM_STATE:(g + 1) * SSM_STATE]
        bg = bmat[:, g * SSM_STATE:(g + 1) * SSM_STATE]
        sl = slice(p * LANES, (p + 1) * LANES)
        xdt_p = xdt[:, sl]
        y = jnp.zeros((q, LANES), F32)
        for k in range(2):
            h = 2 * p + k
            seg = acs_col[:, h:h + 1] - acs_row[h:h + 1, :]
            decay = jnp.exp(jnp.where(causal, seg, NEG))
            scores = (cb[g] * decay).astype(BF16)
            in_head = (lane >= k * HEAD_DIM) & (lane < (k + 1) * HEAD_DIM)
            y = y + _dot(scores, jnp.where(in_head, xdt_p, 0.0).astype(BF16))
        prev = state_ref[p]
        y = y + _dot_nt(cg, prev.astype(BF16)) * ea_x[:, sl]
        contrib = _dot_tn((xdt_p * dstate_x[:, sl]).astype(BF16), bg)
        cd = jnp.exp(jnp.where(sub < HEAD_DIM, alast_col[2 * p:2 * p + 1, :],
                               alast_col[2 * p + 1:2 * p + 2, :]))
        state_ref[p] = prev * cd + contrib
        y = y + dskip_ref[:, sl] * xs[:, sl]
        y_ref[:, sl] = y * _silu(z_ref[:, sl])

    gw = WIDTH // SSM_GROUPS
    for g in range(SSM_GROUPS):
        yg = y_ref[:, g * gw:(g + 1) * gw]
        y_ref[:, g * gw:(g + 1) * gw] = _rms(yg, nw_ref[:, g * gw:(g + 1) * gw])


def _ssd_branch(u, t, conv_w, conv_b, dt_bias, a_log, d_skip, ssm_norm, bsz, seq):
    n = u.shape[0]
    q = SSM_CHUNK
    nc = seq // q
    cch = conv_w.shape[1]
    pad_row = lambda v: jnp.zeros((1, LANES), F32).at[0, :N_HEADS].set(v)
    expand = (jnp.arange(LANES)[:, None] == (jnp.arange(WIDTH)[None, :] // HEAD_DIM)).astype(F32)
    tril = jnp.tril(jnp.ones((q, q), F32))
    const = lambda b, c: (0, 0)
    row = lambda b, c: (b * nc + c, 0)
    return pl.pallas_call(
        _ssd_kernel,
        grid=(bsz, nc),
        in_specs=[
            pl.BlockSpec((q, WIDTH), lambda b, c: (b * nc + c, COL_Z * LANES // WIDTH)),
            pl.BlockSpec((q, cch), lambda b, c: (b * nc + c, COL_XBC * LANES // cch)),
            pl.BlockSpec((q, LANES), lambda b, c: (b * nc + c, COL_DT)),
            pl.BlockSpec((2 * N_HEADS, q), lambda b, c: (0, b * nc + c)),
            pl.BlockSpec((SSM_CONV, cch), const),
            pl.BlockSpec((1, cch), const),
            pl.BlockSpec((1, LANES), const),
            pl.BlockSpec((N_HEADS, 1), const),
            pl.BlockSpec((1, LANES), const),
            pl.BlockSpec((N_HEADS, 1), const),
            pl.BlockSpec((1, WIDTH), const),
            pl.BlockSpec((1, WIDTH), const),
            pl.BlockSpec((q, q), const),
            pl.BlockSpec((q, q), const),
            pl.BlockSpec((LANES, WIDTH), const),
        ],
        out_specs=pl.BlockSpec((q, WIDTH), row),
        out_shape=jax.ShapeDtypeStruct((n, WIDTH), F32),
        scratch_shapes=[pltpu.VMEM((q + 8, cch), F32),
                        pltpu.VMEM((WIDTH // LANES, LANES, SSM_STATE), F32)],
        compiler_params=_params(("arbitrary", "arbitrary")),
        name="ssd_branch",
    )(u, u, u, t, conv_w, conv_b.reshape(1, cch), pad_row(dt_bias), dt_bias.reshape(N_HEADS, 1),
      pad_row(a_log), a_log.reshape(N_HEADS, 1), jnp.repeat(d_skip, HEAD_DIM).reshape(1, WIDTH),
      ssm_norm.reshape(1, WIDTH), tril, tril.T, expand)


ATT_TQ = 512
ATT_TK = 512


def _head_lanes(shape, k):
    lane = lax.broadcasted_iota(jnp.int32, shape, len(shape) - 1)
    return (lane >= k * HEAD_DIM) & (lane < (k + 1) * HEAD_DIM)


def _online_softmax(s, v_bf, m_ref, l_ref, acc_ref, h):
    m_old = m_ref[h]
    m_new = jnp.maximum(m_old, jnp.max(s, axis=-1, keepdims=True))
    alpha = jnp.exp(m_old - m_new)
    p = jnp.exp(s - m_new)
    l_ref[h] = alpha * l_ref[h] + jnp.sum(p, axis=-1, keepdims=True)
    acc_ref[h] = alpha * acc_ref[h] + _dot(p.astype(BF16), v_bf)
    m_ref[h] = m_new


def _init_softmax(m_ref, l_ref, acc_ref):
    m_ref[...] = jnp.full(m_ref.shape, NEG, F32)
    l_ref[...] = jnp.zeros(l_ref.shape, F32)
    acc_ref[...] = jnp.zeros(acc_ref.shape, F32)


def _pair_output(l_ref, acc_ref, p):
    a0 = acc_ref[2 * p] * (1.0 / l_ref[2 * p])
    a1 = acc_ref[2 * p + 1] * (1.0 / l_ref[2 * p + 1])
    return jnp.where(_head_lanes(a0.shape, 0), a0, a1)


def _fox_kernel(q_ref, k_ref, v_ref, cf0_ref, cf1_ref, o_ref, m_ref, l_ref, acc_ref):
    qi, kj = pl.program_id(2), pl.program_id(3)
    tq, tk = q_ref.shape[0], k_ref.shape[0]

    @pl.when(kj == 0)
    def _():
        _init_softmax(m_ref, l_ref, acc_ref)

    def step(diagonal):
        q = q_ref[...]
        kb = k_ref[...].astype(BF16)
        vb = v_ref[...].astype(BF16)
        for k, cf_ref in enumerate((cf0_ref, cf1_ref)):
            qh = jnp.where(_head_lanes(q.shape, k), q, 0.0).astype(BF16)
            s = _dot_nt(qh, kb) - cf_ref[0]
            if diagonal:
                row = lax.broadcasted_iota(jnp.int32, (tq, tk), 0)
                col = lax.broadcasted_iota(jnp.int32, (tq, tk), 1)
                s = jnp.where(col <= row, s, NEG)
            _online_softmax(s, vb, m_ref, l_ref, acc_ref, k)

    pl.when(kj < qi)(functools.partial(step, False))
    pl.when(kj == qi)(functools.partial(step, True))

    @pl.when(kj == pl.num_programs(3) - 1)
    def _():
        o_ref[...] = _pair_output(l_ref, acc_ref, 0)


def _fox_branch(u, t, bsz, seq):
    n = u.shape[0]
    tq = tk = min(ATT_TQ, seq)
    nq, nk = seq // tq, seq // tk
    t3 = t.reshape(2 * N_HEADS, 1, n)
    kv = lambda col: (lambda b, p, i, j: (b * nk + jnp.minimum(j, i), col + p))
    cf = lambda k: (lambda b, p, i, j: (N_HEADS + 2 * p + k, 0, b * nk + jnp.minimum(j, i)))
    return pl.pallas_call(
        _fox_kernel,
        grid=(bsz, WIDTH // LANES, nq, nk),
        in_specs=[
            pl.BlockSpec((tq, LANES), lambda b, p, i, j: (b * nq + i, COL_FQ + p)),
            pl.BlockSpec((tk, LANES), kv(COL_FK)),
            pl.BlockSpec((tk, LANES), kv(COL_FV)),
            pl.BlockSpec((1, 1, tk), cf(0)),
            pl.BlockSpec((1, 1, tk), cf(1)),
        ],
        out_specs=pl.BlockSpec((tq, LANES), lambda b, p, i, j: (b * nq + i, p)),
        out_shape=jax.ShapeDtypeStruct((n, WIDTH), F32),
        scratch_shapes=[pltpu.VMEM((2, tq, 1), F32), pltpu.VMEM((2, tq, 1), F32),
                        pltpu.VMEM((2, tq, LANES), F32)],
        compiler_params=_params(("arbitrary",) * 4),
        name="fox_attention",
    )(u, u, u, t3, t3)


def _moba_kernel(q_ref, k_ref, v_ref, km_ref, o_ref, m_ref, l_ref, acc_ref, sel_ref):
    qi, kj = pl.program_id(2), pl.program_id(3)
    tq, tk = q_ref.shape[0], k_ref.shape[0]
    blocks_per_tile = tq // MOBA_BLOCK
    row = lax.broadcasted_iota(jnp.int32, (tq, 1), 0)
    cur = qi * blocks_per_tile + row // MOBA_BLOCK

    @pl.when(kj == 0)
    def _():
        _init_softmax(m_ref, l_ref, acc_ref)
        q = q_ref[...]
        kmb = km_ref[0].astype(BF16)
        blk = lax.broadcasted_iota(jnp.int32, (tq, LANES), 1)
        for k in range(2):
            qh = jnp.where(_head_lanes(q.shape, k), q, 0.0).astype(BF16)
            gate = jnp.where(blk < cur, _dot_nt(qh, kmb), -jnp.inf)
            for r in range(MOBA_TOPK):
                mx = jnp.max(gate, axis=-1, keepdims=True)
                idx = jnp.min(jnp.where(gate == mx, blk, LANES), axis=-1, keepdims=True)
                sel_ref[k, r] = jnp.where(idx < cur, idx, -1)
                gate = jnp.where(blk == idx, -jnp.inf, gate)

    @pl.when(kj <= qi * blocks_per_tile + blocks_per_tile - 1)
    def _():
        q = q_ref[...]
        kb = k_ref[...].astype(BF16)
        vb = v_ref[...].astype(BF16)
        qpos = qi * tq + lax.broadcasted_iota(jnp.int32, (tq, tk), 0)
        kpos = kj * tk + lax.broadcasted_iota(jnp.int32, (tq, tk), 1)
        own = (kj == cur) & (kpos <= qpos)
        for k in range(2):
            qh = jnp.where(_head_lanes(q.shape, k), q, 0.0).astype(BF16)
            picked = (sel_ref[k, 0] == kj) | (sel_ref[k, 1] == kj) | (sel_ref[k, 2] == kj)
            s = jnp.where(picked | own, _dot_nt(qh, kb), NEG)
            _online_softmax(s, vb, m_ref, l_ref, acc_ref, k)

    @pl.when(kj == pl.num_programs(3) - 1)
    def _():
        o_ref[...] = _pair_output(l_ref, acc_ref, 0)


def _moba_branch(u, km, bsz, seq):
    n = u.shape[0]
    tq, tk = min(ATT_TQ, seq), MOBA_BLOCK
    nq, nk = seq // tq, seq // tk
    assert nk <= LANES and MOBA_TOPK == 3
    kmp = jnp.zeros((bsz, LANES, WIDTH), F32).at[:, :nk].set(km.reshape(bsz, nk, WIDTH))
    last = lambda i: i * (tq // tk) + tq // tk - 1
    kv = lambda col: (lambda b, p, i, j: (b * nk + jnp.minimum(j, last(i)), col + p))
    return pl.pallas_call(
        _moba_kernel,
        grid=(bsz, WIDTH // LANES, nq, nk),
        in_specs=[
            pl.BlockSpec((tq, LANES), lambda b, p, i, j: (b * nq + i, COL_MQ + p)),
            pl.BlockSpec((tk, LANES), kv(COL_MK)),
            pl.BlockSpec((tk, LANES), kv(COL_MV)),
            pl.BlockSpec((1, LANES, LANES), lambda b, p, i, j: (b, 0, p)),
        ],
        out_specs=pl.BlockSpec((tq, LANES), lambda b, p, i, j: (b * nq + i, p)),
        out_shape=jax.ShapeDtypeStruct((n, WIDTH), F32),
        scratch_shapes=[pltpu.VMEM((2, tq, 1), F32), pltpu.VMEM((2, tq, 1), F32),
                        pltpu.VMEM((2, tq, LANES), F32),
                        pltpu.VMEM((2, MOBA_TOPK, tq, 1), jnp.int32)],
        compiler_params=_params(("arbitrary",) * 4),
        name="moba_attention",
    )(u, u, u, kmp)


DSA_TQ = 256
DSA_TK = 512
INT_MIN = -2 ** 31


def _dsa_kernel(q_ref, k_ref, v_ref, iq_ref, iw_ref, ik_ref, o_ref,
                m_ref, l_ref, acc_ref, key_ref, thr_ref, *, seq_bits):
    qi, kj = pl.program_id(1), pl.program_id(2)
    tq, tk = q_ref.shape[0], k_ref.shape[0]
    n_need = (qi * tq + tq - 1) // tk + 1
    qpos = qi * tq + lax.broadcasted_iota(jnp.int32, (tq, tk), 0)
    lane_pos = lax.broadcasted_iota(jnp.int32, (tq, tk), 1)

    @pl.when(kj == 0)
    def _():
        _init_softmax(m_ref, l_ref, acc_ref)
        iq = iq_ref[...]
        iw = iw_ref[...]

        def score_tile(t, carry):
            kib = ik_ref[pl.ds(pl.multiple_of(t * tk, tk), tk), :].astype(BF16)
            isc = jnp.zeros((tq, tk), F32)
            for h in range(IDX_HEADS):
                blk, k = divmod(h, 2)
                qb = iq[:, blk * LANES:(blk + 1) * LANES]
                qh = jnp.where(_head_lanes(qb.shape, k), qb, 0.0).astype(BF16)
                isc = isc + jnp.maximum(_dot_nt(qh, kib), 0.0) * iw[:, h:h + 1]
            bits = pltpu.bitcast(isc, jnp.int32)
            key = jnp.where(bits < 0, bits ^ jnp.int32(0x7FFFFFFF), bits)
            key_ref[t] = jnp.where(t * tk + lane_pos <= qpos, key, INT_MIN)
            return carry

        lax.fori_loop(0, n_need, score_tile, 0)

        def count(pred):
            def body(t, acc):
                hit = jnp.where(pred(key_ref[t], t), 1.0, 0.0)
                for c in range(tk // LANES):
                    acc = acc + hit[:, c * LANES:(c + 1) * LANES]
                return acc
            acc = lax.fori_loop(0, n_need, body, jnp.zeros((tq, LANES), F32))
            return jnp.sum(acc, axis=-1, keepdims=True)

        def thr_bit(it, thr_u):
            cand_u = thr_u | lax.shift_left(jnp.int32(1), 31 - it)
            cand = cand_u ^ jnp.int32(INT_MIN)
            n_ge = count(lambda key, t: key >= cand)
            return jnp.where(n_ge >= DSA_TOPK, cand_u, thr_u)

        thr_u = lax.fori_loop(0, 32, thr_bit, jnp.zeros((tq, 1), jnp.int32))
        thr = thr_u ^ jnp.int32(INT_MIN)
        need = DSA_TOPK - count(lambda key, t: key > thr)

        def pos_bit(it, last):
            cand = last | lax.shift_left(jnp.int32(1), seq_bits - 1 - it)
            n_before = count(lambda key, t: (key == thr) & (t * tk + lane_pos < cand))
            return jnp.where(n_before < need, cand, last)

        last = lax.fori_loop(0, seq_bits, pos_bit, jnp.zeros((tq, 1), jnp.int32))
        thr_ref[0] = thr
        thr_ref[1] = jnp.where(thr == INT_MIN, -1, last)

    @pl.when(kj < n_need)
    def _():
        key = key_ref[kj]
        thr, last = thr_ref[0], thr_ref[1]
        keep = (key > thr) | ((key == thr) & (kj * tk + lane_pos <= last))
        for p in range(WIDTH // LANES):
            sl = slice(p * LANES, (p + 1) * LANES)
            q = q_ref[:, sl]
            kb = k_ref[:, sl].astype(BF16)
            vb = v_ref[:, sl].astype(BF16)
            for k in range(2):
                qh = jnp.where(_head_lanes(q.shape, k), q, 0.0).astype(BF16)
                s = jnp.where(keep, _dot_nt(qh, kb), NEG)
                _online_softmax(s, vb, m_ref, l_ref, acc_ref, 2 * p + k)

    @pl.when(kj == pl.num_programs(2) - 1)
    def _():
        for p in range(WIDTH // LANES):
            o_ref[:, p * LANES:(p + 1) * LANES] = _pair_output(l_ref, acc_ref, p)


def _dsa_branch(u, bsz, seq):
    n = u.shape[0]
    tq, tk = min(DSA_TQ, seq), min(DSA_TK, seq)
    nq, nk = seq // tq, seq // tk
    assert seq & (seq - 1) == 0 and DSA_TOPK <= seq // 4
    last = lambda i: (i * tq + tq - 1) // tk
    kv = lambda col: (lambda b, i, j: (b * nk + jnp.minimum(j, last(i)), col * LANES // WIDTH))
    return pl.pallas_call(
        functools.partial(_dsa_kernel, seq_bits=seq.bit_length() - 1),
        grid=(bsz, nq, nk),
        in_specs=[
            pl.BlockSpec((tq, WIDTH), lambda b, i, j: (b * nq + i, COL_DQ * LANES // WIDTH)),
            pl.BlockSpec((tk, WIDTH), kv(COL_DK)),
            pl.BlockSpec((tk, WIDTH), kv(COL_DV)),
            pl.BlockSpec((tq, 2 * LANES), lambda b, i, j: (b * nq + i, COL_IQ // 2)),
            pl.BlockSpec((tq, LANES), lambda b, i, j: (b * nq + i, COL_IW)),
            pl.BlockSpec((seq, LANES), lambda b, i, j: (b, COL_IK)),
        ],
        out_specs=pl.BlockSpec((tq, WIDTH), lambda b, i, j: (b * nq + i, 0)),
        out_shape=jax.ShapeDtypeStruct((n, WIDTH), F32),
        scratch_shapes=[pltpu.VMEM((N_HEADS, tq, 1), F32), pltpu.VMEM((N_HEADS, tq, 1), F32),
                        pltpu.VMEM((N_HEADS, tq, LANES), F32),
                        pltpu.VMEM((nk, tq, tk), jnp.int32),
                        pltpu.VMEM((2, tq, 1), jnp.int32)],
        compiler_params=_params(("arbitrary",) * 3),
        name="dsa_attention",
    )(u, u, u, u, u, u)


def _merge_kernel(x_ref, gpre_ref, y0_ref, y1_ref, y2_ref, y3_ref, wg_ref, bg_ref, wb_ref, wo_ref,
                  gpost_ref, o_ref):
    x = x_ref[...]
    d = x.shape[1]
    hb = _rms(x, gpre_ref[...]).astype(BF16)
    merged = jnp.zeros(x.shape, F32)
    for n, y_ref in enumerate((y0_ref, y1_ref, y2_ref, y3_ref)):
        logit = _dot(hb, wg_ref[:, n * d:(n + 1) * d]) + bg_ref[:, n * d:(n + 1) * d]
        gate = 1.0 / (1.0 + jnp.exp(-logit))
        merged = merged + gate * _dot(y_ref[...].astype(BF16), wb_ref[n])
    mix = _dot(merged.astype(BF16), wo_ref[...])
    o_ref[...] = x + _rms(mix, gpost_ref[...])


def _merge(x2, g_pre, ys, w_gate, b_gate, w_branch, w_out, g_post):
    n, d = x2.shape
    tm = ROW_TILE
    const2 = lambda i: (0, 0)
    row = lambda i: (i, 0)
    return pl.pallas_call(
        _merge_kernel,
        grid=(n // tm,),
        in_specs=[pl.BlockSpec((tm, d), row), pl.BlockSpec((1, d), const2)]
        + [pl.BlockSpec((tm, WIDTH), row)] * N_BRANCH
        + [pl.BlockSpec((d, N_BRANCH * d), const2), pl.BlockSpec((1, N_BRANCH * d), const2),
           pl.BlockSpec((N_BRANCH, WIDTH, d), lambda i: (0, 0, 0)), pl.BlockSpec((d, d), const2),
           pl.BlockSpec((1, d), const2)],
        out_specs=pl.BlockSpec((tm, d), row),
        out_shape=jax.ShapeDtypeStruct((n, d), F32),
        compiler_params=_params(("arbitrary",)),
        name="branch_merge",
    )(x2, g_pre.reshape(1, d), *ys, w_gate.astype(BF16), b_gate.reshape(1, -1),
      w_branch.astype(BF16), w_out.astype(BF16), g_post.reshape(1, d))


FFN_CHUNK = 256


def _ffn_kernel(x_ref, gpre_ref, wi_ref, wo_ref, gpost_ref, o_ref):
    x = x_ref[...]
    hb = _rms(x, gpre_ref[...]).astype(BF16)
    acc = jnp.zeros(x.shape, F32)
    for c in range(FFN_HIDDEN // FFN_CHUNK):
        lo = c * FFN_CHUNK
        gate = _dot(hb, wi_ref[:, lo:lo + FFN_CHUNK])
        up = _dot(hb, wi_ref[:, FFN_HIDDEN + lo:FFN_HIDDEN + lo + FFN_CHUNK])
        acc = acc + _dot((_silu(gate) * up).astype(BF16), wo_ref[lo:lo + FFN_CHUNK, :])
    o_ref[...] = x + _rms(acc, gpost_ref[...])


def _ffn(x2, g_pre, w_ffn_in, w_ffn_out, g_post):
    n, d = x2.shape
    tm = ROW_TILE
    const2 = lambda i: (0, 0)
    row = lambda i: (i, 0)
    return pl.pallas_call(
        _ffn_kernel,
        grid=(n // tm,),
        in_specs=[pl.BlockSpec((tm, d), row), pl.BlockSpec((1, d), const2),
                  pl.BlockSpec((d, 2 * FFN_HIDDEN), const2), pl.BlockSpec((FFN_HIDDEN, d), const2),
                  pl.BlockSpec((1, d), const2)],
        out_specs=pl.BlockSpec((tm, d), row),
        out_shape=jax.ShapeDtypeStruct((n, d), F32),
        compiler_params=_params(("arbitrary",)),
        name="swiglu_ffn",
    )(x2, g_pre.reshape(1, d), w_ffn_in.astype(BF16), w_ffn_out.astype(BF16), g_post.reshape(1, d))


def kernel(x, norm_mix_pre, w_in, conv_w, conv_b, dt_bias, a_log, d_skip, ssm_norm, fox_fbias, w_gate, b_gate, w_branch, w_out, norm_mix_post, norm_ffn_pre, w_ffn_in, w_ffn_out, norm_ffn_post):
    bsz, seq, d = x.shape
    x2 = x.reshape(bsz * seq, d)
    cos, sin = _rope_tables(seq)
    for l in range(w_in.shape[0]):
        w, wt = _rearranged_w_in(w_in[l])
        u, t, km = _in_projection(x2, norm_mix_pre[l], w, wt, cos, sin, fox_fbias[l], seq)
        ys = (_ssd_branch(u, t, conv_w[l], conv_b[l], dt_bias[l], a_log[l], d_skip[l], ssm_norm[l], bsz, seq),
              _fox_branch(u, t, bsz, seq),
              _moba_branch(u, km, bsz, seq),
              _dsa_branch(u, bsz, seq))
        x2 = _merge(x2, norm_mix_pre[l], ys, w_gate[l], b_gate[l], w_branch[l], w_out[l], norm_mix_post[l])
        x2 = _ffn(x2, norm_ffn_pre[l], w_ffn_in[l], w_ffn_out[l], norm_ffn_post[l])
    return x2.reshape(bsz, seq, d)
```

```python
import functools
import math

import jax
import jax.numpy as jnp
from jax import lax
from jax.experimental import pallas as pl
from jax.experimental.pallas import tpu as pltpu

F32 = jnp.float32
BF16 = jnp.bfloat16

D_MODEL = 1024
HEAD_DIM = 64
N_HEADS = 8
WIDTH = N_HEADS * HEAD_DIM
N_BRANCH = 4
SSM_GROUPS = 2
SSM_STATE = 128
SSM_CONV = 4
SSM_CHUNK = 256
MOBA_BLOCK = 256
MOBA_TOPK = 3
IDX_HEADS = 4
IDX_DIM = 64
DSA_TOPK = 256
ROPE_THETA = 10000.0
FFN_HIDDEN = -(-8 * D_MODEL // (3 * 256)) * 256
EPS = 1e-6

LANES = 128
NEG = -1e30
VMEM_LIMIT = 56 * 1024 * 1024

LOG2E = math.log2(math.e)

A_XBC, A_Z, A_DT, A_IW, NA = 0, 8, 12, 13, 14
B_DV, B_FQ, B_FK, B_FV, B_MV = 0, 4, 8, 12, 16
B_DQ, B_DK, B_MQ, B_MK, B_IQ, B_IK, NB = 20, 24, 28, 32, 36, 38, 40
B_ROPE = B_DQ
N_COLBLK = NA + NB
PROJ_TN = 256
ROW_TILE = 256


def _dot(a, b):
    return jnp.dot(a, b, preferred_element_type=F32)


def _dot_nt(a, b):
    return lax.dot_general(a, b, (((1,), (1,)), ((), ())), preferred_element_type=F32)


def _dot_tn(a, b):
    return lax.dot_general(a, b, (((0,), (0,)), ((), ())), preferred_element_type=F32)


def _dot_f32(a, b):
    return jnp.dot(a, b, preferred_element_type=F32, precision=lax.Precision.HIGHEST)


def _rms(x, g):
    return x * lax.rsqrt(jnp.mean(x * x, axis=-1, keepdims=True) + EPS) * g


def _silu(x):
    return x * (1.0 / (1.0 + jnp.exp(-x)))


def _softplus(x):
    return jnp.maximum(x, 0.0) + jnp.log(1.0 + jnp.exp(-jnp.abs(x)))


def _params(sem):
    return pltpu.CompilerParams(dimension_semantics=sem, vmem_limit_bytes=VMEM_LIMIT)


def _inproj_kernel(x_ref, g_ref, w_ref, wt_ref, cos_ref, sin_ref, fb_ref, tri_ref,
                   ua_ref, ub_ref, t_ref, km_ref, carry_ref, *, tiles_per_seq):
    i = pl.program_id(0)
    hb = _rms(x_ref[...], g_ref[...]).astype(BF16)
    tm = hb.shape[0]
    lane = lax.broadcasted_iota(jnp.int32, (tm, PROJ_TN), 1)
    first_half = (lane % HEAD_DIM) < (HEAD_DIM // 2)
    n_tiles = N_COLBLK * LANES // PROJ_TN
    for j in range(n_tiles):
        c0 = j * PROJ_TN
        acc = _dot(hb, w_ref[:, c0:c0 + PROJ_TN])
        blk = c0 // LANES
        if blk < NA:
            ua_ref[:, c0:c0 + PROJ_TN] = acc
            continue
        blk -= NA
        if blk >= B_ROPE:
            partner = jnp.where(first_half,
                                pltpu.roll(acc, PROJ_TN - HEAD_DIM // 2, 1),
                                pltpu.roll(acc, HEAD_DIM // 2, 1))
            cos = jnp.concatenate([cos_ref[...]] * (PROJ_TN // LANES), axis=1)
            sin = jnp.concatenate([sin_ref[...]] * (PROJ_TN // LANES), axis=1)
            acc = acc * cos + partner * sin
            if B_MK <= blk < B_MK + WIDTH // LANES:
                o = (blk - B_MK) * LANES
                km_ref[0, :, o:o + PROJ_TN] = jnp.mean(acc, axis=0, keepdims=True)
        ub_ref[:, blk * LANES:blk * LANES + PROJ_TN] = acc.astype(BF16)

    t = _dot_nt(wt_ref[...], hb)

    @pl.when(i % tiles_per_seq == 0)
    def _():
        carry_ref[...] = jnp.zeros_like(carry_ref)

    logf = -_softplus(-(t[N_HEADS:] + fb_ref[...]))
    cf = _dot_f32(logf, tri_ref[...]) + carry_ref[...]
    carry_ref[...] = cf[:, tm - 1:tm]
    t_ref[0:N_HEADS, :] = t[:N_HEADS]
    t_ref[N_HEADS:, :] = cf * LOG2E


def _rearranged_w_in(w_in):
    o = 0
    parts = {}
    for name, wdt in (("z", 512), ("xs", 512), ("bs", 256), ("cs", 256), ("dt", 8),
                      ("fq", 512), ("fk", 512), ("fv", 512), ("ff", 8),
                      ("mq", 512), ("mk", 512), ("mv", 512),
                      ("dq", 512), ("dk", 512), ("dv", 512),
                      ("iq", 256), ("ik", 64), ("iw", 4)):
        parts[name] = w_in[:, o:o + wdt]
        o += wdt
    d = w_in.shape[0]
    pad = lambda n: jnp.zeros((d, n), w_in.dtype)
    qs = HEAD_DIM ** -0.5 * LOG2E
    cols = [parts["xs"], parts["bs"], parts["cs"], parts["z"],
            parts["dt"], pad(LANES - 8),
            parts["iw"] * (IDX_HEADS ** -0.5), pad(LANES - 4),
            parts["dv"], parts["fq"] * qs, parts["fk"], parts["fv"], parts["mv"],
            parts["dq"] * qs, parts["dk"], parts["mq"] * qs, parts["mk"],
            parts["iq"] * (IDX_DIM ** -0.5), parts["ik"], parts["ik"], pad(LANES)]
    w = jnp.concatenate(cols, axis=1).astype(BF16)
    assert w.shape[1] == N_COLBLK * LANES
    wt = jnp.concatenate([parts["dt"], parts["ff"]], axis=1).T.astype(BF16)
    return w, wt


def _rope_tables(seq):
    pos = jnp.arange(seq, dtype=F32)
    inv = ROPE_THETA ** (-jnp.arange(0, HEAD_DIM, 2, dtype=F32) / HEAD_DIM)
    ang = pos[:, None] * inv[None, :]
    c, s = jnp.cos(ang), jnp.sin(ang)
    cos = jnp.concatenate([c, c, c, c], axis=1)
    sin = jnp.concatenate([-s, s, -s, s], axis=1)
    return cos, sin


def _in_projection(x2, g, w, wt, cos, sin, fbias, seq):
    n, d = x2.shape
    tm = ROW_TILE
    tiles_per_seq = seq // tm
    ncol = N_COLBLK * LANES
    tri = jnp.triu(jnp.ones((tm, tm), F32))
    const = lambda i: (0, 0)
    return pl.pallas_call(
        functools.partial(_inproj_kernel, tiles_per_seq=tiles_per_seq),
        grid=(n // tm,),
        in_specs=[
            pl.BlockSpec((tm, d), lambda i: (i, 0)),
            pl.BlockSpec((1, d), const),
            pl.BlockSpec((d, ncol), const),
            pl.BlockSpec((2 * N_HEADS, d), const),
            pl.BlockSpec((tm, LANES), lambda i: (i % tiles_per_seq, 0)),
            pl.BlockSpec((tm, LANES), lambda i: (i % tiles_per_seq, 0)),
            pl.BlockSpec((N_HEADS, 1), const),
            pl.BlockSpec((tm, tm), const),
        ],
        out_specs=[
            pl.BlockSpec((tm, NA * LANES), lambda i: (i, 0)),
            pl.BlockSpec((tm, NB * LANES), lambda i: (i, 0)),
            pl.BlockSpec((2 * N_HEADS, tm), lambda i: (0, i)),
            pl.BlockSpec((1, 1, WIDTH), lambda i: (i, 0, 0)),
        ],
        out_shape=[
            jax.ShapeDtypeStruct((n, NA * LANES), F32),
            jax.ShapeDtypeStruct((n, NB * LANES), BF16),
            jax.ShapeDtypeStruct((2 * N_HEADS, n), F32),
            jax.ShapeDtypeStruct((n // tm, 1, WIDTH), F32),
        ],
        scratch_shapes=[pltpu.VMEM((N_HEADS, 1), F32)],
        compiler_params=_params(("arbitrary",)),
        name="in_projection",
    )(x2, g.reshape(1, d), w, wt, cos, sin, fbias.reshape(N_HEADS, 1), tri)


def _ssd_kernel(z_ref, xbc_ref, dtc_ref, t_ref, cw_ref, cb_ref, dtb_row_ref, dtb_col_ref,
                alog_row_ref, alog_col_ref, dskip_ref, nw_ref, tril_ref, triu_ref, expand_ref,
                y_ref, ext_ref, state_ref):
    c = pl.program_id(1)
    q = SSM_CHUNK
    halo = 8
    n_pair = WIDTH // LANES

    @pl.when(c == 0)
    def _():
        ext_ref[0:halo, :] = jnp.zeros((halo, ext_ref.shape[1]), F32)
        state_ref[...] = jnp.zeros_like(state_ref)

    @pl.when(c > 0)
    def _():
        ext_ref[0:halo, :] = ext_ref[q:q + halo, :]

    ext_ref[halo:halo + q, :] = xbc_ref[...]
    conv = cb_ref[...]
    for w in range(SSM_CONV):
        o = halo - (SSM_CONV - 1) + w
        conv = conv + ext_ref[o:o + q, :] * cw_ref[w:w + 1, :]
    xbc = _silu(conv)
    xs = xbc[:, :WIDTH]
    bmat = xbc[:, WIDTH:WIDTH + SSM_GROUPS * SSM_STATE].astype(BF16)
    cmat = xbc[:, WIDTH + SSM_GROUPS * SSM_STATE:].astype(BF16)

    dt_col = _softplus(dtc_ref[...] + dtb_row_ref[...])
    dt_row = _softplus(t_ref[0:N_HEADS, :] + dtb_col_ref[...])
    a_row = -jnp.exp(alog_row_ref[...])
    a_col = -jnp.exp(alog_col_ref[...])
    acs_col = _dot_f32(tril_ref[...], dt_col * a_row)
    acs_row = _dot_f32(dt_row * a_col, triu_ref[...])
    alast_row = acs_col[q - 1:q, :]
    alast_col = acs_row[:, q - 1:q]

    expand = expand_ref[...]
    xdt = xs * _dot_f32(dt_col, expand)
    dstate_x = _dot_f32(jnp.exp(alast_row - acs_col), expand)
    ea_x = _dot_f32(jnp.exp(acs_col), expand)

    li = lax.broadcasted_iota(jnp.int32, (q, q), 0)
    si = lax.broadcasted_iota(jnp.int32, (q, q), 1)
    causal = li >= si
    lane = lax.broadcasted_iota(jnp.int32, (q, LANES), 1)
    sub = lax.broadcasted_iota(jnp.int32, (LANES, 1), 0)

    cb = []
    for g in range(SSM_GROUPS):
        cg = cmat[:, g * SSM_STATE:(g + 1) * SSM_STATE]
        bg = bmat[:, g * SSM_STATE:(g + 1) * SSM_STATE]
        cb.append(_dot_nt(cg, bg))

    for p in range(n_pair):
        g = (2 * p) // (N_HEADS // SSM_GROUPS)
        cg = cmat[:, g * SSM_STATE:(g + 1) * SSM_STATE]
        bg = bmat[:, g * SSM_STATE:(g + 1) * SSM_STATE]
        sl = slice(p * LANES, (p + 1) * LANES)
        xdt_p = xdt[:, sl]
        y = jnp.zeros((q, LANES), F32)
        for k in range(2):
            h = 2 * p + k
            seg = acs_col[:, h:h + 1] - acs_row[h:h + 1, :]
            decay = jnp.exp(jnp.where(causal, seg, NEG))
            scores = (cb[g] * decay).astype(BF16)
            in_head = (lane >= k * HEAD_DIM) & (lane < (k + 1) * HEAD_DIM)
            y = y + _dot(scores, jnp.where(in_head, xdt_p, 0.0).astype(BF16))
        prev = state_ref[p]
        y = y + _dot_nt(cg, prev.astype(BF16)) * ea_x[:, sl]
        contrib = _dot_tn((xdt_p * dstate_x[:, sl]).astype(BF16), bg)
        cd = jnp.exp(jnp.where(sub < HEAD_DIM, alast_col[2 * p:2 * p + 1, :],
                               alast_col[2 * p + 1:2 * p + 2, :]))
        state_ref[p] = prev * cd + contrib
        y = y + dskip_ref[:, sl] * xs[:, sl]
        y_ref[:, sl] = y * _silu(z_ref[:, sl])

    gw = WIDTH // SSM_GROUPS
    for g in range(SSM_GROUPS):
        yg = y_ref[:, g * gw:(g + 1) * gw]
        y_ref[:, g * gw:(g + 1) * gw] = _rms(yg, nw_ref[:, g * gw:(g + 1) * gw])


def _ssd_branch(u, t, conv_w, conv_b, dt_bias, a_log, d_skip, ssm_norm, bsz, seq):
    n = u.shape[0]
    q = SSM_CHUNK
    nc = seq // q
    cch = conv_w.shape[1]
    pad_row = lambda v: jnp.zeros((1, LANES), F32).at[0, :N_HEADS].set(v)
    expand = (jnp.arange(LANES)[:, None] == (jnp.arange(WIDTH)[None, :] // HEAD_DIM)).astype(F32)
    tril = jnp.tril(jnp.ones((q, q), F32))
    const = lambda b, c: (0, 0)
    row = lambda b, c: (b * nc + c, 0)
    return pl.pallas_call(
        _ssd_kernel,
        grid=(bsz, nc),
        in_specs=[
            pl.BlockSpec((q, WIDTH), lambda b, c: (b * nc + c, A_Z * LANES // WIDTH)),
            pl.BlockSpec((q, cch), lambda b, c: (b * nc + c, A_XBC * LANES // cch)),
            pl.BlockSpec((q, LANES), lambda b, c: (b * nc + c, A_DT)),
            pl.BlockSpec((2 * N_HEADS, q), lambda b, c: (0, b * nc + c)),
            pl.BlockSpec((SSM_CONV, cch), const),
            pl.BlockSpec((1, cch), const),
            pl.BlockSpec((1, LANES), const),
            pl.BlockSpec((N_HEADS, 1), const),
            pl.BlockSpec((1, LANES), const),
            pl.BlockSpec((N_HEADS, 1), const),
            pl.BlockSpec((1, WIDTH), const),
            pl.BlockSpec((1, WIDTH), const),
            pl.BlockSpec((q, q), const),
            pl.BlockSpec((q, q), const),
            pl.BlockSpec((LANES, WIDTH), const),
        ],
        out_specs=pl.BlockSpec((q, WIDTH), row),
        out_shape=jax.ShapeDtypeStruct((n, WIDTH), F32),
        scratch_shapes=[pltpu.VMEM((q + 8, cch), F32),
                        pltpu.VMEM((WIDTH // LANES, LANES, SSM_STATE), F32)],
        compiler_params=_params(("arbitrary", "arbitrary")),
        name="ssd_branch",
    )(u, u, u, t, conv_w, conv_b.reshape(1, cch), pad_row(dt_bias), dt_bias.reshape(N_HEADS, 1),
      pad_row(a_log), a_log.reshape(N_HEADS, 1), jnp.repeat(d_skip, HEAD_DIM).reshape(1, WIDTH),
      ssm_norm.reshape(1, WIDTH), tril, tril.T, expand)


ATT_TQ = 512
ATT_TK = 512


def _head_lanes(shape, k):
    lane = lax.broadcasted_iota(jnp.int32, shape, len(shape) - 1)
    return (lane >= k * HEAD_DIM) & (lane < (k + 1) * HEAD_DIM)


def _head_queries(q):
    return [jnp.where(_head_lanes(q.shape, k), q, jnp.zeros_like(q)) for k in range(2)]


def _online_softmax(s, v_bf, m_ref, l_ref, acc_ref, h):
    reps = s.shape[1] // LANES
    m_old = m_ref[h]
    m_new = jnp.maximum(m_old, jnp.max(s, axis=-1, keepdims=True))
    alpha = jnp.exp2(m_old - m_new)
    p = jnp.exp2(s - jnp.concatenate([m_new] * reps, axis=1))
    l_ref[h] = alpha * l_ref[h] + jnp.sum(p, axis=-1, keepdims=True)
    acc_ref[h] = alpha * acc_ref[h] + _dot(p.astype(BF16), v_bf)
    m_ref[h] = m_new


def _init_softmax(m_ref, l_ref, acc_ref):
    m_ref[...] = jnp.full(m_ref.shape, NEG, F32)
    l_ref[...] = jnp.zeros(l_ref.shape, F32)
    acc_ref[...] = jnp.zeros(acc_ref.shape, F32)


def _pair_output(l_ref, acc_ref, p):
    a0 = acc_ref[2 * p] * (1.0 / l_ref[2 * p])
    a1 = acc_ref[2 * p + 1] * (1.0 / l_ref[2 * p + 1])
    return jnp.where(_head_lanes(a0.shape, 0), a0, a1)


def _fox_kernel(q_ref, k_ref, v_ref, cf0_ref, cf1_ref, o_ref, m_ref, l_ref, acc_ref, *, tk):
    qi = pl.program_id(2)
    tq = q_ref.shape[0]
    _init_softmax(m_ref, l_ref, acc_ref)
    qh = _head_queries(q_ref[...])

    def tile(j, diagonal):
        rows = pl.ds(pl.multiple_of(j * tk, tk), tk)
        kb = k_ref[rows, :]
        vb = v_ref[rows, :]
        for k, cf_ref in enumerate((cf0_ref, cf1_ref)):
            s = _dot_nt(qh[k], kb) - cf_ref[0, j]
            if diagonal:
                row = lax.broadcasted_iota(jnp.int32, (tq, tk), 0)
                col = lax.broadcasted_iota(jnp.int32, (tq, tk), 1)
                s = jnp.where(col <= row, s, NEG)
            _online_softmax(s, vb, m_ref, l_ref, acc_ref, k)

    def body(j, carry):
        tile(j, False)
        return carry

    lax.fori_loop(0, qi, body, 0)
    tile(qi, True)
    o_ref[...] = _pair_output(l_ref, acc_ref, 0)


def _fox_branch(ub, t, bsz, seq):
    n = ub.shape[0]
    tq = tk = min(ATT_TQ, seq)
    nq, nk = seq // tq, seq // tk
    t4 = t.reshape(2 * N_HEADS, bsz * nk, 1, tk)
    kv = lambda col: (lambda b, p, i: (b, col + p))
    cf = lambda k: (lambda b, p, i: (N_HEADS + 2 * p + k, b, 0, 0))
    return pl.pallas_call(
        functools.partial(_fox_kernel, tk=tk),
        grid=(bsz, WIDTH // LANES, nq),
        in_specs=[
            pl.BlockSpec((tq, LANES), lambda b, p, i: (b * nq + i, B_FQ + p)),
            pl.BlockSpec((seq, LANES), kv(B_FK)),
            pl.BlockSpec((seq, LANES), kv(B_FV)),
            pl.BlockSpec((1, nk, 1, tk), cf(0)),
            pl.BlockSpec((1, nk, 1, tk), cf(1)),
        ],
        out_specs=pl.BlockSpec((tq, LANES), lambda b, p, i: (b * nq + i, p)),
        out_shape=jax.ShapeDtypeStruct((n, WIDTH), F32),
        scratch_shapes=[pltpu.VMEM((2, tq, LANES), F32), pltpu.VMEM((2, tq, LANES), F32),
                        pltpu.VMEM((2, tq, LANES), F32)],
        compiler_params=_params(("arbitrary",) * 3),
        name="fox_attention",
    )(ub, ub, ub, t4, t4)


BLOCK_MASK = -30000.0


def _moba_kernel(q_ref, k_ref, v_ref, km_ref, o_ref, m_ref, l_ref, acc_ref, *, tk):
    qi = pl.program_id(2)
    tq = q_ref.shape[0]
    blocks_per_tile = tq // tk
    _init_softmax(m_ref, l_ref, acc_ref)
    q = q_ref[...]
    qh = _head_queries(q)
    row = lax.broadcasted_iota(jnp.int32, (tq, LANES), 0)
    blk = lax.broadcasted_iota(jnp.int32, (tq, LANES), 1)
    cur = qi * blocks_per_tile + row // tk
    kmb = km_ref[0].astype(BF16)
    lane_k = lax.broadcasted_iota(jnp.int32, (tk, LANES), 1)

    mask_base = (HEAD_DIM, 0)
    qm = []
    for k in range(2):
        gate = jnp.where(blk < cur, _dot_nt(qh[k], kmb), -jnp.inf)
        picked = blk == cur
        for _ in range(MOBA_TOPK):
            mx = jnp.max(gate, axis=-1, keepdims=True)
            idx = jnp.min(jnp.where(gate == mx, blk, LANES), axis=-1, keepdims=True)
            hit = blk == idx
            picked = picked | (hit & (blk < cur))
            gate = jnp.where(hit, -jnp.inf, gate)
        bias = jnp.where(picked, 0.0, BLOCK_MASK)
        if mask_base[k]:
            bias = pltpu.roll(bias, mask_base[k], 1)
        qm.append(jnp.where(_head_lanes(q.shape, k), q, bias.astype(BF16)))

    def tile(j, own_sub):
        rows = pl.ds(pl.multiple_of(j * tk, tk), tk)
        kb = k_ref[rows, :]
        vb = v_ref[rows, :]
        for k in range(2):
            onehot = jnp.where(lane_k == mask_base[k] + j, 1.0, 0.0).astype(BF16)
            s = _dot_nt(qm[k], jnp.where(_head_lanes(kb.shape, k), kb, onehot))
            if own_sub is not None:
                r2 = lax.broadcasted_iota(jnp.int32, (tq, tk), 0)
                c2 = lax.broadcasted_iota(jnp.int32, (tq, tk), 1)
                s = jnp.where((r2 // tk == own_sub) & (c2 > r2 - own_sub * tk), NEG, s)
            _online_softmax(s, vb, m_ref, l_ref, acc_ref, k)

    def body(j, carry):
        tile(j, None)
        return carry

    lax.fori_loop(0, qi * blocks_per_tile, body, 0)
    for r in range(blocks_per_tile):
        tile(qi * blocks_per_tile + r, r)
    o_ref[...] = _pair_output(l_ref, acc_ref, 0)


def _moba_branch(ub, km, bsz, seq):
    n = ub.shape[0]
    tq, tk = min(ATT_TQ, seq), MOBA_BLOCK
    nq, nk = seq // tq, seq // tk
    assert nk <= HEAD_DIM and MOBA_TOPK == 3
    kmp = jnp.zeros((bsz, LANES, WIDTH), F32).at[:, :nk].set(km.reshape(bsz, nk, WIDTH))
    kv = lambda col: (lambda b, p, i: (b, col + p))
    return pl.pallas_call(
        functools.partial(_moba_kernel, tk=tk),
        grid=(bsz, WIDTH // LANES, nq),
        in_specs=[
            pl.BlockSpec((tq, LANES), lambda b, p, i: (b * nq + i, B_MQ + p)),
            pl.BlockSpec((seq, LANES), kv(B_MK)),
            pl.BlockSpec((seq, LANES), kv(B_MV)),
            pl.BlockSpec((1, LANES, LANES), lambda b, p, i: (b, 0, p)),
        ],
        out_specs=pl.BlockSpec((tq, LANES), lambda b, p, i: (b * nq + i, p)),
        out_shape=jax.ShapeDtypeStruct((n, WIDTH), F32),
        scratch_shapes=[pltpu.VMEM((2, tq, LANES), F32), pltpu.VMEM((2, tq, LANES), F32),
                        pltpu.VMEM((2, tq, LANES), F32)],
        compiler_params=_params(("arbitrary",) * 3),
        name="moba_attention",
    )(ub, ub, ub, kmp)


DSA_TQ = 256
DSA_TK = 512
INT_MIN = -2 ** 31


def _dsa_kernel(q_ref, k_ref, v_ref, iq_ref, iw_ref, ik_ref, o_ref,
                m_ref, l_ref, acc_ref, key_ref, *, tk, seq_bits):
    qi = pl.program_id(1)
    tq = q_ref.shape[0]
    chunks = tk // LANES
    n_need = (qi * tq + tq - 1) // tk + 1
    qpos = qi * tq + lax.broadcasted_iota(jnp.int32, (tq, tk), 0)
    lane_pos = lax.broadcasted_iota(jnp.int32, (tq, tk), 1)
    lane128 = lax.broadcasted_iota(jnp.int32, (tq, LANES), 1)
    rep = lambda x: jnp.broadcast_to(x, (tq, LANES))
    wide = lambda x: jnp.concatenate([x] * chunks, axis=1)

    _init_softmax(m_ref, l_ref, acc_ref)
    iq = iq_ref[...]
    iw = iw_ref[...]
    iqh = [_head_queries(iq[:, b * LANES:(b + 1) * LANES]) for b in range(IDX_HEADS // 2)]

    def score_tile(t, carry):
        kib = ik_ref[pl.ds(pl.multiple_of(t * tk, tk), tk), :]
        isc = jnp.zeros((tq, tk), F32)
        for h in range(IDX_HEADS):
            isc = isc + jnp.maximum(_dot_nt(iqh[h // 2][h % 2], kib), 0.0) * iw[:, h:h + 1]
        bits = pltpu.bitcast(isc, jnp.int32)
        key = jnp.where(bits < 0, bits ^ jnp.int32(0x7FFFFFFF), bits)
        key_ref[t] = jnp.where(t * tk + lane_pos <= qpos, key, INT_MIN)
        return carry

    lax.fori_loop(0, n_need, score_tile, 0)

    def count(pred):
        def body(t, acc):
            kt = key_ref[t]
            for c in range(chunks):
                hit = pred(kt[:, c * LANES:(c + 1) * LANES], t * tk + c * LANES + lane128)
                acc = acc + jnp.where(hit, 1.0, 0.0)
            return acc
        acc = lax.fori_loop(0, n_need, body, jnp.zeros((tq, LANES), F32))
        return rep(jnp.sum(acc, axis=-1, keepdims=True))

    def thr_bit(it, thr_u):
        cand_u = thr_u | lax.shift_left(jnp.int32(1), 31 - it)
        cand = cand_u ^ jnp.int32(INT_MIN)
        n_ge = count(lambda key, pos: key >= cand)
        return jnp.where(n_ge >= DSA_TOPK, cand_u, thr_u)

    thr_u = lax.fori_loop(0, 32, thr_bit, jnp.zeros((tq, LANES), jnp.int32))
    thr = thr_u ^ jnp.int32(INT_MIN)
    need = DSA_TOPK - count(lambda key, pos: key > thr)

    def pos_bit(it, last):
        cand = last | lax.shift_left(jnp.int32(1), seq_bits - 1 - it)
        n_before = count(lambda key, pos: (key == thr) & (pos < cand))
        return jnp.where(n_before < need, cand, last)

    last = lax.fori_loop(0, seq_bits, pos_bit, jnp.zeros((tq, LANES), jnp.int32))
    last = jnp.where(thr == INT_MIN, -1, last)
    thr_w, last_w = wide(thr), wide(last)

    def attend(j, carry):
        rows = pl.ds(pl.multiple_of(j * tk, tk), tk)
        key = key_ref[j]
        keep = (key > thr_w) | ((key == thr_w) & (j * tk + lane_pos <= last_w))
        for p in range(WIDTH // LANES):
            sl = slice(p * LANES, (p + 1) * LANES)
            qh = _head_queries(q_ref[:, sl])
            kb = k_ref[rows, sl]
            vb = v_ref[rows, sl]
            for k in range(2):
                s = jnp.where(keep, _dot_nt(qh[k], kb), NEG)
                _online_softmax(s, vb, m_ref, l_ref, acc_ref, 2 * p + k)
        return carry

    lax.fori_loop(0, n_need, attend, 0)
    for p in range(WIDTH // LANES):
        o_ref[:, p * LANES:(p + 1) * LANES] = _pair_output(l_ref, acc_ref, p)


def _dsa_branch(ua, ub, bsz, seq):
    n = ub.shape[0]
    tq, tk = min(DSA_TQ, seq), min(DSA_TK, seq)
    nq, nk = seq // tq, seq // tk
    assert seq & (seq - 1) == 0 and DSA_TOPK <= seq // 4
    once = pl.Buffered(1)
    return pl.pallas_call(
        functools.partial(_dsa_kernel, tk=tk, seq_bits=seq.bit_length() - 1),
        grid=(bsz, nq),
        in_specs=[
            pl.BlockSpec((tq, WIDTH), lambda b, i: (b * nq + i, B_DQ * LANES // WIDTH)),
            pl.BlockSpec((seq, WIDTH), lambda b, i: (b, B_DK * LANES // WIDTH), pipeline_mode=once),
            pl.BlockSpec((seq, WIDTH), lambda b, i: (b, B_DV * LANES // WIDTH), pipeline_mode=once),
            pl.BlockSpec((tq, 2 * LANES), lambda b, i: (b * nq + i, B_IQ // 2)),
            pl.BlockSpec((tq, LANES), lambda b, i: (b * nq + i, A_IW)),
            pl.BlockSpec((seq, LANES), lambda b, i: (b, B_IK), pipeline_mode=once),
        ],
        out_specs=pl.BlockSpec((tq, WIDTH), lambda b, i: (b * nq + i, 0)),
        out_shape=jax.ShapeDtypeStruct((n, WIDTH), F32),
        scratch_shapes=[pltpu.VMEM((N_HEADS, tq, LANES), F32), pltpu.VMEM((N_HEADS, tq, LANES), F32),
                        pltpu.VMEM((N_HEADS, tq, LANES), F32),
                        pltpu.VMEM((nk, tq, tk), jnp.int32)],
        compiler_params=_params(("arbitrary",) * 2),
        name="dsa_attention",
    )(ub, ub, ub, ub, ua, ub)


def _merge_kernel(x_ref, gpre_ref, y0_ref, y1_ref, y2_ref, y3_ref, wg_ref, bg_ref, wb_ref, wo_ref,
                  gpost_ref, o_ref):
    x = x_ref[...]
    d = x.shape[1]
    hb = _rms(x, gpre_ref[...]).astype(BF16)
    merged = jnp.zeros(x.shape, F32)
    for n, y_ref in enumerate((y0_ref, y1_ref, y2_ref, y3_ref)):
        logit = _dot(hb, wg_ref[:, n * d:(n + 1) * d]) + bg_ref[:, n * d:(n + 1) * d]
        gate = 1.0 / (1.0 + jnp.exp(-logit))
        merged = merged + gate * _dot(y_ref[...].astype(BF16), wb_ref[n])
    mix = _dot(merged.astype(BF16), wo_ref[...])
    o_ref[...] = x + _rms(mix, gpost_ref[...])


def _merge(x2, g_pre, ys, w_gate, b_gate, w_branch, w_out, g_post):
    n, d = x2.shape
    tm = ROW_TILE
    const2 = lambda i: (0, 0)
    row = lambda i: (i, 0)
    return pl.pallas_call(
        _merge_kernel,
        grid=(n // tm,),
        in_specs=[pl.BlockSpec((tm, d), row), pl.BlockSpec((1, d), const2)]
        + [pl.BlockSpec((tm, WIDTH), row)] * N_BRANCH
        + [pl.BlockSpec((d, N_BRANCH * d), const2), pl.BlockSpec((1, N_BRANCH * d), const2),
           pl.BlockSpec((N_BRANCH, WIDTH, d), lambda i: (0, 0, 0)), pl.BlockSpec((d, d), const2),
           pl.BlockSpec((1, d), const2)],
        out_specs=pl.BlockSpec((tm, d), row),
        out_shape=jax.ShapeDtypeStruct((n, d), F32),
        compiler_params=_params(("arbitrary",)),
        name="branch_merge",
    )(x2, g_pre.reshape(1, d), *ys, w_gate.astype(BF16), b_gate.reshape(1, -1),
      w_branch.astype(BF16), w_out.astype(BF16), g_post.reshape(1, d))


FFN_CHUNK = 256


def _ffn_kernel(x_ref, gpre_ref, wi_ref, wo_ref, gpost_ref, o_ref):
    x = x_ref[...]
    hb = _rms(x, gpre_ref[...]).astype(BF16)
    acc = jnp.zeros(x.shape, F32)
    for c in range(FFN_HIDDEN // FFN_CHUNK):
        lo = c * FFN_CHUNK
        gate = _dot(hb, wi_ref[:, lo:lo + FFN_CHUNK])
        up = _dot(hb, wi_ref[:, FFN_HIDDEN + lo:FFN_HIDDEN + lo + FFN_CHUNK])
        acc = acc + _dot((_silu(gate) * up).astype(BF16), wo_ref[lo:lo + FFN_CHUNK, :])
    o_ref[...] = x + _rms(acc, gpost_ref[...])


def _ffn(x2, g_pre, w_ffn_in, w_ffn_out, g_post):
    n, d = x2.shape
    tm = ROW_TILE
    const2 = lambda i: (0, 0)
    row = lambda i: (i, 0)
    return pl.pallas_call(
        _ffn_kernel,
        grid=(n // tm,),
        in_specs=[pl.BlockSpec((tm, d), row), pl.BlockSpec((1, d), const2),
                  pl.BlockSpec((d, 2 * FFN_HIDDEN), const2), pl.BlockSpec((FFN_HIDDEN, d), const2),
                  pl.BlockSpec((1, d), const2)],
        out_specs=pl.BlockSpec((tm, d), row),
        out_shape=jax.ShapeDtypeStruct((n, d), F32),
        compiler_params=_params(("arbitrary",)),
        name="swiglu_ffn",
    )(x2, g_pre.reshape(1, d), w_ffn_in.astype(BF16), w_ffn_out.astype(BF16), g_post.reshape(1, d))


def kernel(x, norm_mix_pre, w_in, conv_w, conv_b, dt_bias, a_log, d_skip, ssm_norm, fox_fbias, w_gate, b_gate, w_branch, w_out, norm_mix_post, norm_ffn_pre, w_ffn_in, w_ffn_out, norm_ffn_post):
    bsz, seq, d = x.shape
    x2 = x.reshape(bsz * seq, d)
    cos, sin = _rope_tables(seq)
    for l in range(w_in.shape[0]):
        w, wt = _rearranged_w_in(w_in[l])
        ua, ub, t, km = _in_projection(x2, norm_mix_pre[l], w, wt, cos, sin, fox_fbias[l], seq)
        ys = (_ssd_branch(ua, t, conv_w[l], conv_b[l], dt_bias[l], a_log[l], d_skip[l], ssm_norm[l], bsz, seq),
              _fox_branch(ub, t, bsz, seq),
              _moba_branch(ub, km, bsz, seq),
              _dsa_branch(ua, ub, bsz, seq))
        x2 = _merge(x2, norm_mix_pre[l], ys, w_gate[l], b_gate[l], w_branch[l], w_out[l], norm_mix_post[l])
        x2 = _ffn(x2, norm_ffn_pre[l], w_ffn_in[l], w_ffn_out[l], norm_ffn_post[l])
    return x2.reshape(bsz, seq, d)
```

```python
import functools
import math

import jax
import jax.numpy as jnp
from jax import lax
from jax.experimental import pallas as pl
from jax.experimental.pallas import tpu as pltpu

F32 = jnp.float32
BF16 = jnp.bfloat16

D_MODEL = 1024
HEAD_DIM = 64
N_HEADS = 8
WIDTH = N_HEADS * HEAD_DIM
N_BRANCH = 4
SSM_GROUPS = 2
SSM_STATE = 128
SSM_CONV = 4
SSM_CHUNK = 256
MOBA_BLOCK = 256
MOBA_TOPK = 3
IDX_HEADS = 4
IDX_DIM = 64
DSA_TOPK = 256
ROPE_THETA = 10000.0
FFN_HIDDEN = -(-8 * D_MODEL // (3 * 256)) * 256
EPS = 1e-6

LANES = 128
NEG = -1e30
VMEM_LIMIT = 56 * 1024 * 1024

LOG2E = math.log2(math.e)

A_XBC, A_Z, A_DT, A_IW, NA = 0, 8, 12, 13, 14
B_DV, B_FQ, B_FK, B_FV, B_MV = 0, 4, 8, 12, 16
B_DQ, B_DK, B_MQ, B_MK, B_IQ, B_IK, NB = 20, 24, 28, 32, 36, 38, 40
B_ROPE = B_DQ
N_COLBLK = NA + NB
PROJ_TN = 256
ROW_TILE = 256


def _dot(a, b):
    return jnp.dot(a, b, preferred_element_type=F32)


def _dot_nt(a, b):
    return lax.dot_general(a, b, (((1,), (1,)), ((), ())), preferred_element_type=F32)


def _dot_tn(a, b):
    return lax.dot_general(a, b, (((0,), (0,)), ((), ())), preferred_element_type=F32)


def _dot_f32(a, b):
    return jnp.dot(a, b, preferred_element_type=F32, precision=lax.Precision.HIGHEST)


def _rms(x, g):
    return x * lax.rsqrt(jnp.mean(x * x, axis=-1, keepdims=True) + EPS) * g


def _silu(x):
    return x * (1.0 / (1.0 + jnp.exp(-x)))


def _softplus(x):
    return jnp.maximum(x, 0.0) + jnp.log(1.0 + jnp.exp(-jnp.abs(x)))


def _params(sem):
    return pltpu.CompilerParams(dimension_semantics=sem, vmem_limit_bytes=VMEM_LIMIT)


def _inproj_kernel(x_ref, g_ref, w_ref, wt_ref, cos_ref, sin_ref, fb_ref, tri_ref,
                   ua_ref, ub_ref, t_ref, km_ref, carry_ref, *, tiles_per_seq):
    i = pl.program_id(0)
    hb = _rms(x_ref[...], g_ref[...]).astype(BF16)
    tm = hb.shape[0]
    lane = lax.broadcasted_iota(jnp.int32, (tm, PROJ_TN), 1)
    first_half = (lane % HEAD_DIM) < (HEAD_DIM // 2)
    n_tiles = N_COLBLK * LANES // PROJ_TN
    for j in range(n_tiles):
        c0 = j * PROJ_TN
        acc = _dot(hb, w_ref[:, c0:c0 + PROJ_TN])
        blk = c0 // LANES
        if blk < NA:
            ua_ref[:, c0:c0 + PROJ_TN] = acc
            continue
        blk -= NA
        if blk >= B_ROPE:
            partner = jnp.where(first_half,
                                pltpu.roll(acc, PROJ_TN - HEAD_DIM // 2, 1),
                                pltpu.roll(acc, HEAD_DIM // 2, 1))
            cos = jnp.concatenate([cos_ref[...]] * (PROJ_TN // LANES), axis=1)
            sin = jnp.concatenate([sin_ref[...]] * (PROJ_TN // LANES), axis=1)
            acc = acc * cos + partner * sin
            if B_MK <= blk < B_MK + WIDTH // LANES:
                o = (blk - B_MK) * LANES
                km_ref[0, :, o:o + PROJ_TN] = jnp.mean(acc, axis=0, keepdims=True)
        ub_ref[:, blk * LANES:blk * LANES + PROJ_TN] = acc.astype(BF16)

    t = _dot_nt(wt_ref[...], hb)

    @pl.when(i % tiles_per_seq == 0)
    def _():
        carry_ref[...] = jnp.zeros_like(carry_ref)

    logf = -_softplus(-(t[N_HEADS:] + fb_ref[...]))
    cf = _dot_f32(logf, tri_ref[...]) + carry_ref[...]
    carry_ref[...] = cf[:, tm - 1:tm]
    t_ref[0:N_HEADS, :] = t[:N_HEADS]
    t_ref[N_HEADS:, :] = cf * LOG2E


def _rearranged_w_in(w_in):
    o = 0
    parts = {}
    for name, wdt in (("z", 512), ("xs", 512), ("bs", 256), ("cs", 256), ("dt", 8),
                      ("fq", 512), ("fk", 512), ("fv", 512), ("ff", 8),
                      ("mq", 512), ("mk", 512), ("mv", 512),
                      ("dq", 512), ("dk", 512), ("dv", 512),
                      ("iq", 256), ("ik", 64), ("iw", 4)):
        parts[name] = w_in[:, o:o + wdt]
        o += wdt
    d = w_in.shape[0]
    pad = lambda n: jnp.zeros((d, n), w_in.dtype)
    qs = HEAD_DIM ** -0.5 * LOG2E
    cols = [parts["xs"], parts["bs"], parts["cs"], parts["z"],
            parts["dt"], pad(LANES - 8),
            parts["iw"] * (IDX_HEADS ** -0.5), pad(LANES - 4),
            parts["dv"], parts["fq"] * qs, parts["fk"], parts["fv"], parts["mv"],
            parts["dq"] * qs, parts["dk"], parts["mq"] * qs, parts["mk"],
            parts["iq"] * (IDX_DIM ** -0.5), parts["ik"], parts["ik"], pad(LANES)]
    w = jnp.concatenate(cols, axis=1).astype(BF16)
    assert w.shape[1] == N_COLBLK * LANES
    wt = jnp.concatenate([parts["dt"], parts["ff"]], axis=1).T.astype(BF16)
    return w, wt


def _rope_tables(seq):
    pos = jnp.arange(seq, dtype=F32)
    inv = ROPE_THETA ** (-jnp.arange(0, HEAD_DIM, 2, dtype=F32) / HEAD_DIM)
    ang = pos[:, None] * inv[None, :]
    c, s = jnp.cos(ang), jnp.sin(ang)
    cos = jnp.concatenate([c, c, c, c], axis=1)
    sin = jnp.concatenate([-s, s, -s, s], axis=1)
    return cos, sin


def _in_projection(x2, g, w, wt, cos, sin, fbias, seq):
    n, d = x2.shape
    tm = ROW_TILE
    tiles_per_seq = seq // tm
    ncol = N_COLBLK * LANES
    tri = jnp.triu(jnp.ones((tm, tm), F32))
    const = lambda i: (0, 0)
    return pl.pallas_call(
        functools.partial(_inproj_kernel, tiles_per_seq=tiles_per_seq),
        grid=(n // tm,),
        in_specs=[
            pl.BlockSpec((tm, d), lambda i: (i, 0)),
            pl.BlockSpec((1, d), const),
            pl.BlockSpec((d, ncol), const),
            pl.BlockSpec((2 * N_HEADS, d), const),
            pl.BlockSpec((tm, LANES), lambda i: (i % tiles_per_seq, 0)),
            pl.BlockSpec((tm, LANES), lambda i: (i % tiles_per_seq, 0)),
            pl.BlockSpec((N_HEADS, 1), const),
            pl.BlockSpec((tm, tm), const),
        ],
        out_specs=[
            pl.BlockSpec((tm, NA * LANES), lambda i: (i, 0)),
            pl.BlockSpec((tm, NB * LANES), lambda i: (i, 0)),
            pl.BlockSpec((2 * N_HEADS, tm), lambda i: (0, i)),
            pl.BlockSpec((1, 1, WIDTH), lambda i: (i, 0, 0)),
        ],
        out_shape=[
            jax.ShapeDtypeStruct((n, NA * LANES), F32),
            jax.ShapeDtypeStruct((n, NB * LANES), BF16),
            jax.ShapeDtypeStruct((2 * N_HEADS, n), F32),
            jax.ShapeDtypeStruct((n // tm, 1, WIDTH), F32),
        ],
        scratch_shapes=[pltpu.VMEM((N_HEADS, 1), F32)],
        compiler_params=_params(("arbitrary",)),
        name="in_projection",
    )(x2, g.reshape(1, d), w, wt, cos, sin, fbias.reshape(N_HEADS, 1), tri)


def _ssd_kernel(z_ref, xbc_ref, dtc_ref, t_ref, cw_ref, cb_ref, dtb_row_ref, dtb_col_ref,
                alog_row_ref, alog_col_ref, dskip_ref, nw_ref, tril_ref, triu_ref, expand_ref,
                y_ref, ext_ref, state_ref):
    c = pl.program_id(1)
    q = SSM_CHUNK
    halo = 8
    n_pair = WIDTH // LANES

    @pl.when(c == 0)
    def _():
        ext_ref[0:halo, :] = jnp.zeros((halo, ext_ref.shape[1]), F32)
        state_ref[...] = jnp.zeros_like(state_ref)

    @pl.when(c > 0)
    def _():
        ext_ref[0:halo, :] = ext_ref[q:q + halo, :]

    ext_ref[halo:halo + q, :] = xbc_ref[...]
    conv = cb_ref[...]
    for w in range(SSM_CONV):
        o = halo - (SSM_CONV - 1) + w
        conv = conv + ext_ref[o:o + q, :] * cw_ref[w:w + 1, :]
    xbc = _silu(conv)
    xs = xbc[:, :WIDTH]
    bmat = xbc[:, WIDTH:WIDTH + SSM_GROUPS * SSM_STATE].astype(BF16)
    cmat = xbc[:, WIDTH + SSM_GROUPS * SSM_STATE:].astype(BF16)

    dt_col = _softplus(dtc_ref[...] + dtb_row_ref[...])
    dt_row = _softplus(t_ref[0:N_HEADS, :] + dtb_col_ref[...])
    a_row = -jnp.exp(alog_row_ref[...])
    a_col = -jnp.exp(alog_col_ref[...])
    acs_col = _dot_f32(tril_ref[...], dt_col * a_row)
    acs_row = _dot_f32(dt_row * a_col, triu_ref[...])
    alast_row = acs_col[q - 1:q, :]
    alast_col = acs_row[:, q - 1:q]

    expand = expand_ref[...]
    xdt = xs * _dot_f32(dt_col, expand)
    dstate_x = _dot_f32(jnp.exp(alast_row - acs_col), expand)
    ea_x = _dot_f32(jnp.exp(acs_col), expand)

    li = lax.broadcasted_iota(jnp.int32, (q, q), 0)
    si = lax.broadcasted_iota(jnp.int32, (q, q), 1)
    causal = li >= si
    lane = lax.broadcasted_iota(jnp.int32, (q, LANES), 1)
    sub = lax.broadcasted_iota(jnp.int32, (LANES, 1), 0)

    cb = []
    for g in range(SSM_GROUPS):
        cg = cmat[:, g * SSM_STATE:(g + 1) * SSM_STATE]
        bg = bmat[:, g * SSM_STATE:(g + 1) * SSM_STATE]
        cb.append(_dot_nt(cg, bg))

    for p in range(n_pair):
        g = (2 * p) // (N_HEADS // SSM_GROUPS)
        cg = cmat[:, g * SSM_STATE:(g + 1) * SSM_STATE]
        bg = bmat[:, g * SSM_STATE:(g + 1) * SSM_STATE]
        sl = slice(p * LANES, (p + 1) * LANES)
        xdt_p = xdt[:, sl]
        y = jnp.zeros((q, LANES), F32)
        for k in range(2):
            h = 2 * p + k
            seg = acs_col[:, h:h + 1] - acs_row[h:h + 1, :]
            decay = jnp.exp(jnp.where(causal, seg, NEG))
            scores = (cb[g] * decay).astype(BF16)
            in_head = (lane >= k * HEAD_DIM) & (lane < (k + 1) * HEAD_DIM)
            y = y + _dot(scores, jnp.where(in_head, xdt_p, 0.0).astype(BF16))
        prev = state_ref[p]
        y = y + _dot_nt(cg, prev.astype(BF16)) * ea_x[:, sl]
        contrib = _dot_tn((xdt_p * dstate_x[:, sl]).astype(BF16), bg)
        cd = jnp.exp(jnp.where(sub < HEAD_DIM, alast_col[2 * p:2 * p + 1, :],
                               alast_col[2 * p + 1:2 * p + 2, :]))
        state_ref[p] = prev * cd + contrib
        y = y + dskip_ref[:, sl] * xs[:, sl]
        y_ref[:, sl] = y * _silu(z_ref[:, sl])

    gw = WIDTH // SSM_GROUPS
    for g in range(SSM_GROUPS):
        yg = y_ref[:, g * gw:(g + 1) * gw]
        y_ref[:, g * gw:(g + 1) * gw] = _rms(yg, nw_ref[:, g * gw:(g + 1) * gw])


def _ssd_branch(u, t, conv_w, conv_b, dt_bias, a_log, d_skip, ssm_norm, bsz, seq):
    n = u.shape[0]
    q = SSM_CHUNK
    nc = seq // q
    cch = conv_w.shape[1]
    pad_row = lambda v: jnp.zeros((1, LANES), F32).at[0, :N_HEADS].set(v)
    expand = (jnp.arange(LANES)[:, None] == (jnp.arange(WIDTH)[None, :] // HEAD_DIM)).astype(F32)
    tril = jnp.tril(jnp.ones((q, q), F32))
    const = lambda b, c: (0, 0)
    row = lambda b, c: (b * nc + c, 0)
    return pl.pallas_call(
        _ssd_kernel,
        grid=(bsz, nc),
        in_specs=[
            pl.BlockSpec((q, WIDTH), lambda b, c: (b * nc + c, A_Z * LANES // WIDTH)),
            pl.BlockSpec((q, cch), lambda b, c: (b * nc + c, A_XBC * LANES // cch)),
            pl.BlockSpec((q, LANES), lambda b, c: (b * nc + c, A_DT)),
            pl.BlockSpec((2 * N_HEADS, q), lambda b, c: (0, b * nc + c)),
            pl.BlockSpec((SSM_CONV, cch), const),
            pl.BlockSpec((1, cch), const),
            pl.BlockSpec((1, LANES), const),
            pl.BlockSpec((N_HEADS, 1), const),
            pl.BlockSpec((1, LANES), const),
            pl.BlockSpec((N_HEADS, 1), const),
            pl.BlockSpec((1, WIDTH), const),
            pl.BlockSpec((1, WIDTH), const),
            pl.BlockSpec((q, q), const),
            pl.BlockSpec((q, q), const),
            pl.BlockSpec((LANES, WIDTH), const),
        ],
        out_specs=pl.BlockSpec((q, WIDTH), row),
        out_shape=jax.ShapeDtypeStruct((n, WIDTH), F32),
        scratch_shapes=[pltpu.VMEM((q + 8, cch), F32),
                        pltpu.VMEM((WIDTH // LANES, LANES, SSM_STATE), F32)],
        compiler_params=_params(("arbitrary", "arbitrary")),
        name="ssd_branch",
    )(u, u, u, t, conv_w, conv_b.reshape(1, cch), pad_row(dt_bias), dt_bias.reshape(N_HEADS, 1),
      pad_row(a_log), a_log.reshape(N_HEADS, 1), jnp.repeat(d_skip, HEAD_DIM).reshape(1, WIDTH),
      ssm_norm.reshape(1, WIDTH), tril, tril.T, expand)


ATT_TQ = 512
ATT_TK = 512


def _head_lanes(shape, k):
    lane = lax.broadcasted_iota(jnp.int32, shape, len(shape) - 1)
    return (lane >= k * HEAD_DIM) & (lane < (k + 1) * HEAD_DIM)


def _head_queries(q):
    return [jnp.where(_head_lanes(q.shape, k), q, jnp.zeros_like(q)) for k in range(2)]


def _with_ones(v_bf):
    return jnp.concatenate([v_bf, jnp.ones_like(v_bf)], axis=1)


def _online_softmax(s, v_ext, m_ref, l_ref, acc_ref, h):
    reps = s.shape[1] // LANES
    m_old = m_ref[h]
    m_new = jnp.maximum(m_old, jnp.max(s, axis=-1, keepdims=True))
    alpha = jnp.exp2(m_old - m_new)
    p = jnp.exp2((s - jnp.concatenate([m_new] * reps, axis=1)).astype(BF16))
    pv = _dot(p, v_ext)
    l_ref[h] = alpha * l_ref[h] + pv[:, LANES:]
    acc_ref[h] = alpha * acc_ref[h] + pv[:, :LANES]
    m_ref[h] = m_new


def _init_softmax(m_ref, l_ref, acc_ref):
    m_ref[...] = jnp.full(m_ref.shape, NEG, F32)
    l_ref[...] = jnp.zeros(l_ref.shape, F32)
    acc_ref[...] = jnp.zeros(acc_ref.shape, F32)


def _pair_output(l_ref, acc_ref, p):
    a0 = acc_ref[2 * p] * (1.0 / l_ref[2 * p])
    a1 = acc_ref[2 * p + 1] * (1.0 / l_ref[2 * p + 1])
    return jnp.where(_head_lanes(a0.shape, 0), a0, a1)


def _fox_kernel(q_ref, k_ref, v_ref, cf0_ref, cf1_ref, o_ref, m_ref, l_ref, acc_ref, *, tk):
    qi = pl.program_id(2)
    tq = q_ref.shape[0]
    _init_softmax(m_ref, l_ref, acc_ref)
    qh = _head_queries(q_ref[...])

    def tile(j, diagonal):
        rows = pl.ds(pl.multiple_of(j * tk, tk), tk)
        kb = k_ref[rows, :]
        vb = _with_ones(v_ref[rows, :])
        for k, cf_ref in enumerate((cf0_ref, cf1_ref)):
            s = _dot_nt(qh[k], kb) - cf_ref[0, j]
            if diagonal:
                row = lax.broadcasted_iota(jnp.int32, (tq, tk), 0)
                col = lax.broadcasted_iota(jnp.int32, (tq, tk), 1)
                s = jnp.where(col <= row, s, NEG)
            _online_softmax(s, vb, m_ref, l_ref, acc_ref, k)

    def body(u, carry):
        tile(2 * u, False)
        tile(2 * u + 1, False)
        return carry

    lax.fori_loop(0, qi // 2, body, 0)
    pl.when(qi % 2 == 1)(lambda: tile(qi - 1, False))
    tile(qi, True)
    o_ref[...] = _pair_output(l_ref, acc_ref, 0)


def _fox_branch(ub, t, bsz, seq):
    n = ub.shape[0]
    tq = tk = min(ATT_TQ, seq)
    nq, nk = seq // tq, seq // tk
    t4 = t.reshape(2 * N_HEADS, bsz * nk, 1, tk)
    kv = lambda col: (lambda b, p, i: (b, col + p))
    cf = lambda k: (lambda b, p, i: (N_HEADS + 2 * p + k, b, 0, 0))
    return pl.pallas_call(
        functools.partial(_fox_kernel, tk=tk),
        grid=(bsz, WIDTH // LANES, nq),
        in_specs=[
            pl.BlockSpec((tq, LANES), lambda b, p, i: (b * nq + i, B_FQ + p)),
            pl.BlockSpec((seq, LANES), kv(B_FK)),
            pl.BlockSpec((seq, LANES), kv(B_FV)),
            pl.BlockSpec((1, nk, 1, tk), cf(0)),
            pl.BlockSpec((1, nk, 1, tk), cf(1)),
        ],
        out_specs=pl.BlockSpec((tq, LANES), lambda b, p, i: (b * nq + i, p)),
        out_shape=jax.ShapeDtypeStruct((n, WIDTH), F32),
        scratch_shapes=[pltpu.VMEM((2, tq, LANES), F32), pltpu.VMEM((2, tq, LANES), F32),
                        pltpu.VMEM((2, tq, LANES), F32)],
        compiler_params=_params(("arbitrary",) * 3),
        name="fox_attention",
    )(ub, ub, ub, t4, t4)


BLOCK_MASK = -30000.0


def _moba_kernel(q_ref, k_ref, v_ref, km_ref, o_ref, m_ref, l_ref, acc_ref, *, tk):
    qi = pl.program_id(2)
    tq = q_ref.shape[0]
    blocks_per_tile = tq // MOBA_BLOCK
    _init_softmax(m_ref, l_ref, acc_ref)
    q = q_ref[...]
    qh = _head_queries(q)
    row = lax.broadcasted_iota(jnp.int32, (tq, LANES), 0)
    blk = lax.broadcasted_iota(jnp.int32, (tq, LANES), 1)
    cur = qi * blocks_per_tile + row // MOBA_BLOCK
    kmb = km_ref[0].astype(BF16)
    lane_k = lax.broadcasted_iota(jnp.int32, (tk, LANES), 1)
    blk_k = lax.broadcasted_iota(jnp.int32, (tk, LANES), 0) // MOBA_BLOCK

    mask_base = (HEAD_DIM, 0)
    qm = []
    for k in range(2):
        gate = jnp.where(blk < cur, _dot_nt(qh[k], kmb), -jnp.inf)
        picked = blk == cur
        for _ in range(MOBA_TOPK):
            mx = jnp.max(gate, axis=-1, keepdims=True)
            idx = jnp.min(jnp.where(gate == mx, blk, LANES), axis=-1, keepdims=True)
            hit = blk == idx
            picked = picked | (hit & (blk < cur))
            gate = jnp.where(hit, -jnp.inf, gate)
        bias = jnp.where(picked, 0.0, BLOCK_MASK)
        if mask_base[k]:
            bias = pltpu.roll(bias, mask_base[k], 1)
        qm.append(jnp.where(_head_lanes(q.shape, k), q, bias.astype(BF16)))

    def tile(j, diagonal):
        rows = pl.ds(pl.multiple_of(j * tk, tk), tk)
        kb = k_ref[rows, :]
        vb = _with_ones(v_ref[rows, :])
        for k in range(2):
            onehot = jnp.where(lane_k == mask_base[k] + j * (tk // MOBA_BLOCK) + blk_k, 1.0, 0.0).astype(BF16)
            s = _dot_nt(qm[k], jnp.where(_head_lanes(kb.shape, k), kb, onehot))
            if diagonal:
                r2 = lax.broadcasted_iota(jnp.int32, (tq, tk), 0)
                c2 = lax.broadcasted_iota(jnp.int32, (tq, tk), 1)
                s = jnp.where((r2 // MOBA_BLOCK == c2 // MOBA_BLOCK) & (c2 > r2), NEG, s)
            _online_softmax(s, vb, m_ref, l_ref, acc_ref, k)

    def body(u, carry):
        tile(2 * u, False)
        tile(2 * u + 1, False)
        return carry

    lax.fori_loop(0, qi // 2, body, 0)
    pl.when(qi % 2 == 1)(lambda: tile(qi - 1, False))
    tile(qi, True)
    o_ref[...] = _pair_output(l_ref, acc_ref, 0)


def _moba_branch(ub, km, bsz, seq):
    n = ub.shape[0]
    tq = tk = min(ATT_TQ, seq)
    nq, nblk = seq // tq, seq // MOBA_BLOCK
    assert nblk <= HEAD_DIM and MOBA_TOPK == 3 and tk % MOBA_BLOCK == 0
    kmp = jnp.zeros((bsz, LANES, WIDTH), F32).at[:, :nblk].set(km.reshape(bsz, nblk, WIDTH))
    kv = lambda col: (lambda b, p, i: (b, col + p))
    return pl.pallas_call(
        functools.partial(_moba_kernel, tk=tk),
        grid=(bsz, WIDTH // LANES, nq),
        in_specs=[
            pl.BlockSpec((tq, LANES), lambda b, p, i: (b * nq + i, B_MQ + p)),
            pl.BlockSpec((seq, LANES), kv(B_MK)),
            pl.BlockSpec((seq, LANES), kv(B_MV)),
            pl.BlockSpec((1, LANES, LANES), lambda b, p, i: (b, 0, p)),
        ],
        out_specs=pl.BlockSpec((tq, LANES), lambda b, p, i: (b * nq + i, p)),
        out_shape=jax.ShapeDtypeStruct((n, WIDTH), F32),
        scratch_shapes=[pltpu.VMEM((2, tq, LANES), F32), pltpu.VMEM((2, tq, LANES), F32),
                        pltpu.VMEM((2, tq, LANES), F32)],
        compiler_params=_params(("arbitrary",) * 3),
        name="moba_attention",
    )(ub, ub, ub, kmp)


DSA_TQ = 512
DSA_TK = 512
COUNT_ROWS = 128
INT_MIN = -2 ** 31


def _dsa_kernel(q_ref, k_ref, v_ref, iq_ref, iw_ref, ik_ref, o_ref,
                m_ref, l_ref, acc_ref, key_ref, bias_ref, *, tk, seq_bits):
    qi = pl.program_id(1)
    tq = q_ref.shape[0]
    chunks = tk // LANES
    n_need = (qi * tq + tq - 1) // tk + 1
    qpos = qi * tq + lax.broadcasted_iota(jnp.int32, (tq, tk), 0)
    lane_pos = lax.broadcasted_iota(jnp.int32, (tq, tk), 1)
    wide = lambda x: jnp.concatenate([x] * chunks, axis=1)

    _init_softmax(m_ref, l_ref, acc_ref)
    iq = iq_ref[...]
    iw = iw_ref[...]
    iqh = [_head_queries(iq[:, b * LANES:(b + 1) * LANES]) for b in range(IDX_HEADS // 2)]

    def score_tile(t, carry):
        kib = ik_ref[pl.ds(pl.multiple_of(t * tk, tk), tk), :]
        isc = jnp.zeros((tq, tk), F32)
        for h in range(IDX_HEADS):
            isc = isc + jnp.maximum(_dot_nt(iqh[h // 2][h % 2], kib), 0.0) * iw[:, h:h + 1]
        bits = pltpu.bitcast(isc, jnp.int32)
        key = jnp.where(bits < 0, bits ^ jnp.int32(0x7FFFFFFF), bits)
        key_ref[t] = jnp.where(t * tk + lane_pos <= qpos, key, INT_MIN)
        return carry

    lax.fori_loop(0, n_need, score_tile, 0)

    @pl.when(n_need % 2 == 1)
    def _():
        key_ref[n_need] = jnp.full((tq, tk), INT_MIN, jnp.int32)

    lane_rb = lax.broadcasted_iota(jnp.int32, (COUNT_ROWS, LANES), 1)

    def count(pred, *row_args):
        parts = []
        for rb in range(tq // COUNT_ROWS):
            rs = slice(rb * COUNT_ROWS, (rb + 1) * COUNT_ROWS)
            args = [a[rs] for a in row_args]

            def one(t, acc):
                for c in range(chunks):
                    kc = key_ref[t, rs, c * LANES:(c + 1) * LANES]
                    hit = pred(kc, t * tk + c * LANES + lane_rb, *args)
                    acc = acc + jnp.where(hit, 1.0, 0.0)
                return acc

            acc = lax.fori_loop(0, (n_need + 1) // 2, lambda u, a: one(2 * u + 1, one(2 * u, a)),
                                jnp.zeros((COUNT_ROWS, LANES), F32))
            parts.append(jnp.broadcast_to(jnp.sum(acc, axis=-1, keepdims=True), (COUNT_ROWS, LANES)))
        return jnp.concatenate(parts, axis=0)

    def thr_bit(it, carry):
        thr_u, n_at = carry
        cand_u = thr_u | lax.shift_left(jnp.int32(1), 31 - it)
        n_ge = count(lambda key, pos, c: key >= c, cand_u ^ jnp.int32(INT_MIN))
        take = n_ge >= DSA_TOPK
        return jnp.where(take, cand_u, thr_u), jnp.where(take, n_ge, n_at)

    all_keys = jnp.full((tq, LANES), 2.0 * DSA_TOPK, F32)
    thr_u, n_at = lax.fori_loop(0, 32, thr_bit, (jnp.zeros((tq, LANES), jnp.int32), all_keys))
    thr = thr_u ^ jnp.int32(INT_MIN)
    short = thr == INT_MIN
    excess = jnp.where(short, 0.0, n_at - DSA_TOPK)

    def tie_search():
        need = DSA_TOPK - count(lambda key, pos, c: key > c, thr)

        def pos_bit(it, last):
            cand = last | lax.shift_left(jnp.int32(1), seq_bits - 1 - it)
            n_before = count(lambda key, pos, c, th: (key == th) & (pos < c), cand, thr)
            return jnp.where(n_before < need, cand, last)

        return lax.fori_loop(0, seq_bits, pos_bit, jnp.zeros((tq, LANES), jnp.int32))

    everything = jnp.full((tq, LANES), 2 ** seq_bits, jnp.int32)
    last = lax.cond(jnp.max(excess) > 0.0, tie_search, lambda: everything)
    last = jnp.where(short, -1, last)
    thr_w, last_w = wide(thr), wide(last)

    def attend(j, carry):
        rows = pl.ds(pl.multiple_of(j * tk, tk), tk)
        key = key_ref[j]
        keep = (key > thr_w) | ((key == thr_w) & (j * tk + lane_pos <= last_w))
        bias_ref[...] = jnp.where(keep, 0.0, NEG)
        for p in range(WIDTH // LANES):
            sl = slice(p * LANES, (p + 1) * LANES)
            qh = _head_queries(q_ref[:, sl])
            kb = k_ref[rows, sl]
            vb = _with_ones(v_ref[rows, sl])
            for k in range(2):
                s = _dot_nt(qh[k], kb) + bias_ref[...]
                _online_softmax(s, vb, m_ref, l_ref, acc_ref, 2 * p + k)
        return carry

    lax.fori_loop(0, n_need, attend, 0)
    for p in range(WIDTH // LANES):
        o_ref[:, p * LANES:(p + 1) * LANES] = _pair_output(l_ref, acc_ref, p)


def _dsa_branch(ua, ub, bsz, seq):
    n = ub.shape[0]
    tq, tk = min(DSA_TQ, seq), min(DSA_TK, seq)
    nq, nk = seq // tq, seq // tk
    assert seq & (seq - 1) == 0 and DSA_TOPK <= seq // 4
    once = pl.Buffered(1)
    return pl.pallas_call(
        functools.partial(_dsa_kernel, tk=tk, seq_bits=seq.bit_length() - 1),
        grid=(bsz, nq),
        in_specs=[
            pl.BlockSpec((tq, WIDTH), lambda b, i: (b * nq + i, B_DQ * LANES // WIDTH)),
            pl.BlockSpec((seq, WIDTH), lambda b, i: (b, B_DK * LANES // WIDTH), pipeline_mode=once),
            pl.BlockSpec((seq, WIDTH), lambda b, i: (b, B_DV * LANES // WIDTH), pipeline_mode=once),
            pl.BlockSpec((tq, 2 * LANES), lambda b, i: (b * nq + i, B_IQ // 2)),
            pl.BlockSpec((tq, LANES), lambda b, i: (b * nq + i, A_IW)),
            pl.BlockSpec((seq, LANES), lambda b, i: (b, B_IK), pipeline_mode=once),
        ],
        out_specs=pl.BlockSpec((tq, WIDTH), lambda b, i: (b * nq + i, 0)),
        out_shape=jax.ShapeDtypeStruct((n, WIDTH), F32),
        scratch_shapes=[pltpu.VMEM((N_HEADS, tq, LANES), F32), pltpu.VMEM((N_HEADS, tq, LANES), F32),
                        pltpu.VMEM((N_HEADS, tq, LANES), F32),
                        pltpu.VMEM((nk + nk % 2, tq, tk), jnp.int32),
                        pltpu.VMEM((tq, tk), F32)],
        compiler_params=_params(("arbitrary",) * 2),
        name="dsa_attention",
    )(ub, ub, ub, ub, ua, ub)


def _merge_kernel(x_ref, gpre_ref, y0_ref, y1_ref, y2_ref, y3_ref, wg_ref, bg_ref, wb_ref, wo_ref,
                  gpost_ref, o_ref):
    x = x_ref[...]
    d = x.shape[1]
    hb = _rms(x, gpre_ref[...]).astype(BF16)
    merged = jnp.zeros(x.shape, F32)
    for n, y_ref in enumerate((y0_ref, y1_ref, y2_ref, y3_ref)):
        logit = _dot(hb, wg_ref[:, n * d:(n + 1) * d]) + bg_ref[:, n * d:(n + 1) * d]
        gate = 1.0 / (1.0 + jnp.exp(-logit))
        merged = merged + gate * _dot(y_ref[...].astype(BF16), wb_ref[n])
    mix = _dot(merged.astype(BF16), wo_ref[...])
    o_ref[...] = x + _rms(mix, gpost_ref[...])


def _merge(x2, g_pre, ys, w_gate, b_gate, w_branch, w_out, g_post):
    n, d = x2.shape
    tm = ROW_TILE
    const2 = lambda i: (0, 0)
    row = lambda i: (i, 0)
    return pl.pallas_call(
        _merge_kernel,
        grid=(n // tm,),
        in_specs=[pl.BlockSpec((tm, d), row), pl.BlockSpec((1, d), const2)]
        + [pl.BlockSpec((tm, WIDTH), row)] * N_BRANCH
        + [pl.BlockSpec((d, N_BRANCH * d), const2), pl.BlockSpec((1, N_BRANCH * d), const2),
           pl.BlockSpec((N_BRANCH, WIDTH, d), lambda i: (0, 0, 0)), pl.BlockSpec((d, d), const2),
           pl.BlockSpec((1, d), const2)],
        out_specs=pl.BlockSpec((tm, d), row),
        out_shape=jax.ShapeDtypeStruct((n, d), F32),
        compiler_params=_params(("arbitrary",)),
        name="branch_merge",
    )(x2, g_pre.reshape(1, d), *ys, w_gate.astype(BF16), b_gate.reshape(1, -1),
      w_branch.astype(BF16), w_out.astype(BF16), g_post.reshape(1, d))


FFN_CHUNK = 256


def _ffn_kernel(x_ref, gpre_ref, wi_ref, wo_ref, gpost_ref, o_ref):
    x = x_ref[...]
    hb = _rms(x, gpre_ref[...]).astype(BF16)
    acc = jnp.zeros(x.shape, F32)
    for c in range(FFN_HIDDEN // FFN_CHUNK):
        lo = c * FFN_CHUNK
        gate = _dot(hb, wi_ref[:, lo:lo + FFN_CHUNK])
        up = _dot(hb, wi_ref[:, FFN_HIDDEN + lo:FFN_HIDDEN + lo + FFN_CHUNK])
        acc = acc + _dot((_silu(gate) * up).astype(BF16), wo_ref[lo:lo + FFN_CHUNK, :])
    o_ref[...] = x + _rms(acc, gpost_ref[...])


def _ffn(x2, g_pre, w_ffn_in, w_ffn_out, g_post):
    n, d = x2.shape
    tm = ROW_TILE
    const2 = lambda i: (0, 0)
    row = lambda i: (i, 0)
    return pl.pallas_call(
        _ffn_kernel,
        grid=(n // tm,),
        in_specs=[pl.BlockSpec((tm, d), row), pl.BlockSpec((1, d), const2),
                  pl.BlockSpec((d, 2 * FFN_HIDDEN), const2), pl.BlockSpec((FFN_HIDDEN, d), const2),
                  pl.BlockSpec((1, d), const2)],
        out_specs=pl.BlockSpec((tm, d), row),
        out_shape=jax.ShapeDtypeStruct((n, d), F32),
        compiler_params=_params(("arbitrary",)),
        name="swiglu_ffn",
    )(x2, g_pre.reshape(1, d), w_ffn_in.astype(BF16), w_ffn_out.astype(BF16), g_post.reshape(1, d))


def kernel(x, norm_mix_pre, w_in, conv_w, conv_b, dt_bias, a_log, d_skip, ssm_norm, fox_fbias, w_gate, b_gate, w_branch, w_out, norm_mix_post, norm_ffn_pre, w_ffn_in, w_ffn_out, norm_ffn_post):
    bsz, seq, d = x.shape
    x2 = x.reshape(bsz * seq, d)
    cos, sin = _rope_tables(seq)
    for l in range(w_in.shape[0]):
        w, wt = _rearranged_w_in(w_in[l])
        ua, ub, t, km = _in_projection(x2, norm_mix_pre[l], w, wt, cos, sin, fox_fbias[l], seq)
        ys = (_ssd_branch(ua, t, conv_w[l], conv_b[l], dt_bias[l], a_log[l], d_skip[l], ssm_norm[l], bsz, seq),
              _fox_branch(ub, t, bsz, seq),
              _moba_branch(ub, km, bsz, seq),
              _dsa_branch(ua, ub, bsz, seq))
        x2 = _merge(x2, norm_mix_pre[l], ys, w_gate[l], b_gate[l], w_branch[l], w_out[l], norm_mix_post[l])
        x2 = _ffn(x2, norm_ffn_pre[l], w_ffn_in[l], w_ffn_out[l], norm_ffn_post[l])
    return x2.reshape(bsz, seq, d)
```

```python
import functools
import math

import jax
import jax.numpy as jnp
from jax import lax
from jax.experimental import pallas as pl
from jax.experimental.pallas import tpu as pltpu

F32 = jnp.float32
BF16 = jnp.bfloat16

D_MODEL = 1024
HEAD_DIM = 64
N_HEADS = 8
WIDTH = N_HEADS * HEAD_DIM
N_BRANCH = 4
SSM_GROUPS = 2
SSM_STATE = 128
SSM_CONV = 4
SSM_CHUNK = 256
MOBA_BLOCK = 256
MOBA_TOPK = 3
IDX_HEADS = 4
IDX_DIM = 64
DSA_TOPK = 256
ROPE_THETA = 10000.0
FFN_HIDDEN = -(-8 * D_MODEL // (3 * 256)) * 256
EPS = 1e-6

LANES = 128
NEG = -1e30
VMEM_LIMIT = 56 * 1024 * 1024

LOG2E = math.log2(math.e)

A_XBC, A_Z, A_DT, A_IW, NA = 0, 8, 12, 13, 14
B_DV, B_FQ, B_FK, B_FV, B_MV = 0, 4, 8, 12, 16
B_DQ, B_DK, B_MQ, B_MK, B_IQ, B_IK, NB = 20, 24, 28, 32, 36, 38, 40
B_ROPE = B_DQ
N_COLBLK = NA + NB
PROJ_TN = 256
ROW_TILE = 256


def _dot(a, b):
    return jnp.dot(a, b, preferred_element_type=F32)


def _dot_nt(a, b):
    return lax.dot_general(a, b, (((1,), (1,)), ((), ())), preferred_element_type=F32)


def _dot_tn(a, b):
    return lax.dot_general(a, b, (((0,), (0,)), ((), ())), preferred_element_type=F32)


def _dot_f32(a, b):
    return jnp.dot(a, b, preferred_element_type=F32, precision=lax.Precision.HIGHEST)


def _rms(x, g):
    return x * lax.rsqrt(jnp.mean(x * x, axis=-1, keepdims=True) + EPS) * g


def _silu(x):
    return x * (1.0 / (1.0 + jnp.exp(-x)))


def _softplus(x):
    return jnp.maximum(x, 0.0) + jnp.log(1.0 + jnp.exp(-jnp.abs(x)))


def _params(sem):
    return pltpu.CompilerParams(dimension_semantics=sem, vmem_limit_bytes=VMEM_LIMIT)


def _inproj_kernel(x_ref, g_ref, w_ref, wt_ref, cos_ref, sin_ref, fb_ref, tri_ref,
                   ua_ref, ub_ref, t_ref, km_ref, carry_ref, *, tiles_per_seq):
    i = pl.program_id(0)
    hb = _rms(x_ref[...], g_ref[...]).astype(BF16)
    tm = hb.shape[0]
    lane = lax.broadcasted_iota(jnp.int32, (tm, PROJ_TN), 1)
    first_half = (lane % HEAD_DIM) < (HEAD_DIM // 2)
    n_tiles = N_COLBLK * LANES // PROJ_TN
    for j in range(n_tiles):
        c0 = j * PROJ_TN
        acc = _dot(hb, w_ref[:, c0:c0 + PROJ_TN])
        blk = c0 // LANES
        if blk < NA:
            ua_ref[:, c0:c0 + PROJ_TN] = acc
            continue
        blk -= NA
        if blk >= B_ROPE:
            partner = jnp.where(first_half,
                                pltpu.roll(acc, PROJ_TN - HEAD_DIM // 2, 1),
                                pltpu.roll(acc, HEAD_DIM // 2, 1))
            cos = jnp.concatenate([cos_ref[...]] * (PROJ_TN // LANES), axis=1)
            sin = jnp.concatenate([sin_ref[...]] * (PROJ_TN // LANES), axis=1)
            acc = acc * cos + partner * sin
            if B_MK <= blk < B_MK + WIDTH // LANES:
                o = (blk - B_MK) * LANES
                km_ref[0, :, o:o + PROJ_TN] = jnp.mean(acc, axis=0, keepdims=True)
        ub_ref[:, blk * LANES:blk * LANES + PROJ_TN] = acc.astype(BF16)

    t = _dot_nt(wt_ref[...], hb)

    @pl.when(i % tiles_per_seq == 0)
    def _():
        carry_ref[...] = jnp.zeros_like(carry_ref)

    logf = -_softplus(-(t[N_HEADS:] + fb_ref[...]))
    cf = _dot_f32(logf, tri_ref[...]) + carry_ref[...]
    carry_ref[...] = cf[:, tm - 1:tm]
    t_ref[0:N_HEADS, :] = t[:N_HEADS]
    t_ref[N_HEADS:, :] = cf * LOG2E


def _rearranged_w_in(w_in):
    o = 0
    parts = {}
    for name, wdt in (("z", 512), ("xs", 512), ("bs", 256), ("cs", 256), ("dt", 8),
                      ("fq", 512), ("fk", 512), ("fv", 512), ("ff", 8),
                      ("mq", 512), ("mk", 512), ("mv", 512),
                      ("dq", 512), ("dk", 512), ("dv", 512),
                      ("iq", 256), ("ik", 64), ("iw", 4)):
        parts[name] = w_in[:, o:o + wdt]
        o += wdt
    d = w_in.shape[0]
    pad = lambda n: jnp.zeros((d, n), w_in.dtype)
    qs = HEAD_DIM ** -0.5 * LOG2E
    cols = [parts["xs"], parts["bs"], parts["cs"], parts["z"],
            parts["dt"], pad(LANES - 8),
            parts["iw"] * (IDX_HEADS ** -0.5), pad(LANES - 4),
            parts["dv"], parts["fq"] * qs, parts["fk"], parts["fv"], parts["mv"],
            parts["dq"] * qs, parts["dk"], parts["mq"] * qs, parts["mk"],
            parts["iq"] * (IDX_DIM ** -0.5), parts["ik"], parts["ik"], pad(LANES)]
    w = jnp.concatenate(cols, axis=1).astype(BF16)
    assert w.shape[1] == N_COLBLK * LANES
    wt = jnp.concatenate([parts["dt"], parts["ff"]], axis=1).T.astype(BF16)
    return w, wt


def _rope_tables(seq):
    pos = jnp.arange(seq, dtype=F32)
    inv = ROPE_THETA ** (-jnp.arange(0, HEAD_DIM, 2, dtype=F32) / HEAD_DIM)
    ang = pos[:, None] * inv[None, :]
    c, s = jnp.cos(ang), jnp.sin(ang)
    cos = jnp.concatenate([c, c, c, c], axis=1)
    sin = jnp.concatenate([-s, s, -s, s], axis=1)
    return cos, sin


def _in_projection(x2, g, w, wt, cos, sin, fbias, seq):
    n, d = x2.shape
    tm = ROW_TILE
    tiles_per_seq = seq // tm
    ncol = N_COLBLK * LANES
    tri = jnp.triu(jnp.ones((tm, tm), F32))
    const = lambda i: (0, 0)
    return pl.pallas_call(
        functools.partial(_inproj_kernel, tiles_per_seq=tiles_per_seq),
        grid=(n // tm,),
        in_specs=[
            pl.BlockSpec((tm, d), lambda i: (i, 0)),
            pl.BlockSpec((1, d), const),
            pl.BlockSpec((d, ncol), const),
            pl.BlockSpec((2 * N_HEADS, d), const),
            pl.BlockSpec((tm, LANES), lambda i: (i % tiles_per_seq, 0)),
            pl.BlockSpec((tm, LANES), lambda i: (i % tiles_per_seq, 0)),
            pl.BlockSpec((N_HEADS, 1), const),
            pl.BlockSpec((tm, tm), const),
        ],
        out_specs=[
            pl.BlockSpec((tm, NA * LANES), lambda i: (i, 0)),
            pl.BlockSpec((tm, NB * LANES), lambda i: (i, 0)),
            pl.BlockSpec((2 * N_HEADS, tm), lambda i: (0, i)),
            pl.BlockSpec((1, 1, WIDTH), lambda i: (i, 0, 0)),
        ],
        out_shape=[
            jax.ShapeDtypeStruct((n, NA * LANES), F32),
            jax.ShapeDtypeStruct((n, NB * LANES), BF16),
            jax.ShapeDtypeStruct((2 * N_HEADS, n), F32),
            jax.ShapeDtypeStruct((n // tm, 1, WIDTH), F32),
        ],
        scratch_shapes=[pltpu.VMEM((N_HEADS, 1), F32)],
        compiler_params=_params(("arbitrary",)),
        name="in_projection",
    )(x2, g.reshape(1, d), w, wt, cos, sin, fbias.reshape(N_HEADS, 1), tri)


def _ssd_kernel(z_ref, xbc_ref, dtc_ref, t_ref, cw_ref, cb_ref, dtb_row_ref, dtb_col_ref,
                alog_row_ref, alog_col_ref, dskip_ref, nw_ref, tril_ref, triu_ref, expand_ref,
                y_ref, ext_ref, state_ref):
    c = pl.program_id(1)
    q = SSM_CHUNK
    halo = 8
    n_pair = WIDTH // LANES

    @pl.when(c == 0)
    def _():
        ext_ref[0:halo, :] = jnp.zeros((halo, ext_ref.shape[1]), F32)
        state_ref[...] = jnp.zeros_like(state_ref)

    @pl.when(c > 0)
    def _():
        ext_ref[0:halo, :] = ext_ref[q:q + halo, :]

    ext_ref[halo:halo + q, :] = xbc_ref[...]
    conv = cb_ref[...]
    for w in range(SSM_CONV):
        o = halo - (SSM_CONV - 1) + w
        conv = conv + ext_ref[o:o + q, :] * cw_ref[w:w + 1, :]
    xbc = _silu(conv)
    xs = xbc[:, :WIDTH]
    bmat = xbc[:, WIDTH:WIDTH + SSM_GROUPS * SSM_STATE].astype(BF16)
    cmat = xbc[:, WIDTH + SSM_GROUPS * SSM_STATE:].astype(BF16)

    dt_col = _softplus(dtc_ref[...] + dtb_row_ref[...])
    dt_row = _softplus(t_ref[0:N_HEADS, :] + dtb_col_ref[...])
    a_row = -jnp.exp(alog_row_ref[...])
    a_col = -jnp.exp(alog_col_ref[...])
    acs_col = _dot_f32(tril_ref[...], dt_col * a_row)
    acs_row = _dot_f32(dt_row * a_col, triu_ref[...])
    alast_row = acs_col[q - 1:q, :]
    alast_col = acs_row[:, q - 1:q]

    expand = expand_ref[...]
    xdt = xs * _dot_f32(dt_col, expand)
    dstate_x = _dot_f32(jnp.exp(alast_row - acs_col), expand)
    ea_x = _dot_f32(jnp.exp(acs_col), expand)

    li = lax.broadcasted_iota(jnp.int32, (q, q), 0)
    si = lax.broadcasted_iota(jnp.int32, (q, q), 1)
    causal = li >= si
    lane = lax.broadcasted_iota(jnp.int32, (q, LANES), 1)
    sub = lax.broadcasted_iota(jnp.int32, (LANES, 1), 0)

    cb = []
    for g in range(SSM_GROUPS):
        cg = cmat[:, g * SSM_STATE:(g + 1) * SSM_STATE]
        bg = bmat[:, g * SSM_STATE:(g + 1) * SSM_STATE]
        cb.append(_dot_nt(cg, bg))

    for p in range(n_pair):
        g = (2 * p) // (N_HEADS // SSM_GROUPS)
        cg = cmat[:, g * SSM_STATE:(g + 1) * SSM_STATE]
        bg = bmat[:, g * SSM_STATE:(g + 1) * SSM_STATE]
        sl = slice(p * LANES, (p + 1) * LANES)
        xdt_p = xdt[:, sl]
        y = jnp.zeros((q, LANES), F32)
        for k in range(2):
            h = 2 * p + k
            seg = acs_col[:, h:h + 1] - acs_row[h:h + 1, :]
            decay = jnp.exp(jnp.where(causal, seg, NEG))
            scores = (cb[g] * decay).astype(BF16)
            in_head = (lane >= k * HEAD_DIM) & (lane < (k + 1) * HEAD_DIM)
            y = y + _dot(scores, jnp.where(in_head, xdt_p, 0.0).astype(BF16))
        prev = state_ref[p]
        y = y + _dot_nt(cg, prev.astype(BF16)) * ea_x[:, sl]
        contrib = _dot_tn((xdt_p * dstate_x[:, sl]).astype(BF16), bg)
        cd = jnp.exp(jnp.where(sub < HEAD_DIM, alast_col[2 * p:2 * p + 1, :],
                               alast_col[2 * p + 1:2 * p + 2, :]))
        state_ref[p] = prev * cd + contrib
        y = y + dskip_ref[:, sl] * xs[:, sl]
        y_ref[:, sl] = y * _silu(z_ref[:, sl])

    gw = WIDTH // SSM_GROUPS
    for g in range(SSM_GROUPS):
        yg = y_ref[:, g * gw:(g + 1) * gw]
        y_ref[:, g * gw:(g + 1) * gw] = _rms(yg, nw_ref[:, g * gw:(g + 1) * gw])


def _ssd_branch(u, t, conv_w, conv_b, dt_bias, a_log, d_skip, ssm_norm, bsz, seq):
    n = u.shape[0]
    q = SSM_CHUNK
    nc = seq // q
    cch = conv_w.shape[1]
    pad_row = lambda v: jnp.zeros((1, LANES), F32).at[0, :N_HEADS].set(v)
    expand = (jnp.arange(LANES)[:, None] == (jnp.arange(WIDTH)[None, :] // HEAD_DIM)).astype(F32)
    tril = jnp.tril(jnp.ones((q, q), F32))
    const = lambda b, c: (0, 0)
    row = lambda b, c: (b * nc + c, 0)
    return pl.pallas_call(
        _ssd_kernel,
        grid=(bsz, nc),
        in_specs=[
            pl.BlockSpec((q, WIDTH), lambda b, c: (b * nc + c, A_Z * LANES // WIDTH)),
            pl.BlockSpec((q, cch), lambda b, c: (b * nc + c, A_XBC * LANES // cch)),
            pl.BlockSpec((q, LANES), lambda b, c: (b * nc + c, A_DT)),
            pl.BlockSpec((2 * N_HEADS, q), lambda b, c: (0, b * nc + c)),
            pl.BlockSpec((SSM_CONV, cch), const),
            pl.BlockSpec((1, cch), const),
            pl.BlockSpec((1, LANES), const),
            pl.BlockSpec((N_HEADS, 1), const),
            pl.BlockSpec((1, LANES), const),
            pl.BlockSpec((N_HEADS, 1), const),
            pl.BlockSpec((1, WIDTH), const),
            pl.BlockSpec((1, WIDTH), const),
            pl.BlockSpec((q, q), const),
            pl.BlockSpec((q, q), const),
            pl.BlockSpec((LANES, WIDTH), const),
        ],
        out_specs=pl.BlockSpec((q, WIDTH), row),
        out_shape=jax.ShapeDtypeStruct((n, WIDTH), F32),
        scratch_shapes=[pltpu.VMEM((q + 8, cch), F32),
                        pltpu.VMEM((WIDTH // LANES, LANES, SSM_STATE), F32)],
        compiler_params=_params(("arbitrary", "arbitrary")),
        name="ssd_branch",
    )(u, u, u, t, conv_w, conv_b.reshape(1, cch), pad_row(dt_bias), dt_bias.reshape(N_HEADS, 1),
      pad_row(a_log), a_log.reshape(N_HEADS, 1), jnp.repeat(d_skip, HEAD_DIM).reshape(1, WIDTH),
      ssm_norm.reshape(1, WIDTH), tril, tril.T, expand)


ATT_TQ = 512
ATT_TK = 512


def _head_lanes(shape, k):
    lane = lax.broadcasted_iota(jnp.int32, shape, len(shape) - 1)
    return (lane >= k * HEAD_DIM) & (lane < (k + 1) * HEAD_DIM)


def _head_queries(q):
    return [jnp.where(_head_lanes(q.shape, k), q, jnp.zeros_like(q)) for k in range(2)]


def _with_ones(v_bf):
    return jnp.concatenate([v_bf, jnp.ones_like(v_bf)], axis=1)


def _online_softmax(s, v_ext, m_ref, l_ref, acc_ref, h):
    reps = s.shape[1] // LANES
    m_old = m_ref[h]
    m_new = jnp.maximum(m_old, jnp.max(s, axis=-1, keepdims=True))
    alpha = jnp.exp2(m_old - m_new)
    p = jnp.exp2((s - jnp.concatenate([m_new] * reps, axis=1)).astype(BF16))
    pv = _dot(p, v_ext)
    l_ref[h] = alpha * l_ref[h] + pv[:, LANES:]
    acc_ref[h] = alpha * acc_ref[h] + pv[:, :LANES]
    m_ref[h] = m_new


def _init_softmax(m_ref, l_ref, acc_ref):
    m_ref[...] = jnp.full(m_ref.shape, NEG, F32)
    l_ref[...] = jnp.zeros(l_ref.shape, F32)
    acc_ref[...] = jnp.zeros(acc_ref.shape, F32)


def _pair_output(l_ref, acc_ref, p):
    a0 = acc_ref[2 * p] * (1.0 / l_ref[2 * p])
    a1 = acc_ref[2 * p + 1] * (1.0 / l_ref[2 * p + 1])
    return jnp.where(_head_lanes(a0.shape, 0), a0, a1)


def _fox_kernel(q_ref, k_ref, v_ref, cf0_ref, cf1_ref, o_ref, m_ref, l_ref, acc_ref, *, tk):
    qi = pl.program_id(2)
    tq = q_ref.shape[0]
    _init_softmax(m_ref, l_ref, acc_ref)
    qh = _head_queries(q_ref[...])

    def tile(j, diagonal):
        rows = pl.ds(pl.multiple_of(j * tk, tk), tk)
        kb = k_ref[rows, :]
        vb = _with_ones(v_ref[rows, :])
        for k, cf_ref in enumerate((cf0_ref, cf1_ref)):
            s = _dot_nt(qh[k], kb) - cf_ref[0, j]
            if diagonal:
                row = lax.broadcasted_iota(jnp.int32, (tq, tk), 0)
                col = lax.broadcasted_iota(jnp.int32, (tq, tk), 1)
                s = jnp.where(col <= row, s, NEG)
            _online_softmax(s, vb, m_ref, l_ref, acc_ref, k)

    def body(u, carry):
        tile(2 * u, False)
        tile(2 * u + 1, False)
        return carry

    lax.fori_loop(0, qi // 2, body, 0)
    pl.when(qi % 2 == 1)(lambda: tile(qi - 1, False))
    tile(qi, True)
    o_ref[...] = _pair_output(l_ref, acc_ref, 0)


def _fox_branch(ub, t, bsz, seq):
    n = ub.shape[0]
    tq = tk = min(ATT_TQ, seq)
    nq, nk = seq // tq, seq // tk
    t4 = t.reshape(2 * N_HEADS, bsz * nk, 1, tk)
    kv = lambda col: (lambda b, p, i: (b, col + p))
    cf = lambda k: (lambda b, p, i: (N_HEADS + 2 * p + k, b, 0, 0))
    return pl.pallas_call(
        functools.partial(_fox_kernel, tk=tk),
        grid=(bsz, WIDTH // LANES, nq),
        in_specs=[
            pl.BlockSpec((tq, LANES), lambda b, p, i: (b * nq + i, B_FQ + p)),
            pl.BlockSpec((seq, LANES), kv(B_FK)),
            pl.BlockSpec((seq, LANES), kv(B_FV)),
            pl.BlockSpec((1, nk, 1, tk), cf(0)),
            pl.BlockSpec((1, nk, 1, tk), cf(1)),
        ],
        out_specs=pl.BlockSpec((tq, LANES), lambda b, p, i: (b * nq + i, p)),
        out_shape=jax.ShapeDtypeStruct((n, WIDTH), F32),
        scratch_shapes=[pltpu.VMEM((2, tq, LANES), F32), pltpu.VMEM((2, tq, LANES), F32),
                        pltpu.VMEM((2, tq, LANES), F32)],
        compiler_params=_params(("arbitrary",) * 3),
        name="fox_attention",
    )(ub, ub, ub, t4, t4)


BLOCK_MASK = -30000.0


def _moba_kernel(q_ref, k_ref, v_ref, km_ref, o_ref, m_ref, l_ref, acc_ref, *, tk):
    qi = pl.program_id(2)
    tq = q_ref.shape[0]
    blocks_per_tile = tq // MOBA_BLOCK
    _init_softmax(m_ref, l_ref, acc_ref)
    q = q_ref[...]
    qh = _head_queries(q)
    row = lax.broadcasted_iota(jnp.int32, (tq, LANES), 0)
    blk = lax.broadcasted_iota(jnp.int32, (tq, LANES), 1)
    cur = qi * blocks_per_tile + row // MOBA_BLOCK
    kmb = km_ref[0].astype(BF16)
    lane_k = lax.broadcasted_iota(jnp.int32, (tk, LANES), 1)
    blk_k = lax.broadcasted_iota(jnp.int32, (tk, LANES), 0) // MOBA_BLOCK

    mask_base = (HEAD_DIM, 0)
    qm = []
    for k in range(2):
        gate = jnp.where(blk < cur, _dot_nt(qh[k], kmb), -jnp.inf)
        picked = blk == cur
        for _ in range(MOBA_TOPK):
            mx = jnp.max(gate, axis=-1, keepdims=True)
            idx = jnp.min(jnp.where(gate == mx, blk, LANES), axis=-1, keepdims=True)
            hit = blk == idx
            picked = picked | (hit & (blk < cur))
            gate = jnp.where(hit, -jnp.inf, gate)
        bias = jnp.where(picked, 0.0, BLOCK_MASK)
        if mask_base[k]:
            bias = pltpu.roll(bias, mask_base[k], 1)
        qm.append(jnp.where(_head_lanes(q.shape, k), q, bias.astype(BF16)))

    def tile(j, diagonal):
        rows = pl.ds(pl.multiple_of(j * tk, tk), tk)
        kb = k_ref[rows, :]
        vb = _with_ones(v_ref[rows, :])
        for k in range(2):
            onehot = jnp.where(lane_k == mask_base[k] + j * (tk // MOBA_BLOCK) + blk_k, 1.0, 0.0).astype(BF16)
            s = _dot_nt(qm[k], jnp.where(_head_lanes(kb.shape, k), kb, onehot))
            if diagonal:
                r2 = lax.broadcasted_iota(jnp.int32, (tq, tk), 0)
                c2 = lax.broadcasted_iota(jnp.int32, (tq, tk), 1)
                s = jnp.where((r2 // MOBA_BLOCK == c2 // MOBA_BLOCK) & (c2 > r2), NEG, s)
            _online_softmax(s, vb, m_ref, l_ref, acc_ref, k)

    def body(u, carry):
        tile(2 * u, False)
        tile(2 * u + 1, False)
        return carry

    lax.fori_loop(0, qi // 2, body, 0)
    pl.when(qi % 2 == 1)(lambda: tile(qi - 1, False))
    tile(qi, True)
    o_ref[...] = _pair_output(l_ref, acc_ref, 0)


def _moba_branch(ub, km, bsz, seq):
    n = ub.shape[0]
    tq = tk = min(ATT_TQ, seq)
    nq, nblk = seq // tq, seq // MOBA_BLOCK
    assert nblk <= HEAD_DIM and MOBA_TOPK == 3 and tk % MOBA_BLOCK == 0
    kmp = jnp.zeros((bsz, LANES, WIDTH), F32).at[:, :nblk].set(km.reshape(bsz, nblk, WIDTH))
    kv = lambda col: (lambda b, p, i: (b, col + p))
    return pl.pallas_call(
        functools.partial(_moba_kernel, tk=tk),
        grid=(bsz, WIDTH // LANES, nq),
        in_specs=[
            pl.BlockSpec((tq, LANES), lambda b, p, i: (b * nq + i, B_MQ + p)),
            pl.BlockSpec((seq, LANES), kv(B_MK)),
            pl.BlockSpec((seq, LANES), kv(B_MV)),
            pl.BlockSpec((1, LANES, LANES), lambda b, p, i: (b, 0, p)),
        ],
        out_specs=pl.BlockSpec((tq, LANES), lambda b, p, i: (b * nq + i, p)),
        out_shape=jax.ShapeDtypeStruct((n, WIDTH), F32),
        scratch_shapes=[pltpu.VMEM((2, tq, LANES), F32), pltpu.VMEM((2, tq, LANES), F32),
                        pltpu.VMEM((2, tq, LANES), F32)],
        compiler_params=_params(("arbitrary",) * 3),
        name="moba_attention",
    )(ub, ub, ub, kmp)


DSA_TQ = 512
DSA_TK = 512
COUNT_ROWS = 128
INT_MIN = -2 ** 31


def _dsa_kernel(q_ref, k_ref, v_ref, iq_ref, iw_ref, ik_ref, o_ref,
                m_ref, l_ref, acc_ref, key_ref, bias_ref, *, tk, seq_bits):
    qi = pl.program_id(1)
    tq = q_ref.shape[0]
    chunks = tk // LANES
    n_need = (qi * tq + tq - 1) // tk + 1
    qpos = qi * tq + lax.broadcasted_iota(jnp.int32, (tq, tk), 0)
    lane_pos = lax.broadcasted_iota(jnp.int32, (tq, tk), 1)
    wide = lambda x: jnp.concatenate([x] * chunks, axis=1)

    _init_softmax(m_ref, l_ref, acc_ref)
    iq = iq_ref[...]
    iw = iw_ref[...]
    iqh = [_head_queries(iq[:, b * LANES:(b + 1) * LANES]) for b in range(IDX_HEADS // 2)]

    zero_keys = jnp.int32(2 ** seq_bits)

    def score_tile(t, carry):
        kib = ik_ref[pl.ds(pl.multiple_of(t * tk, tk), tk), :]
        isc = jnp.zeros((tq, tk), F32)
        for h in range(IDX_HEADS):
            isc = isc + jnp.maximum(_dot_nt(iqh[h // 2][h % 2], kib), 0.0) * iw[:, h:h + 1]
        kpos = t * tk + lane_pos
        bits = pltpu.bitcast(isc, jnp.int32)
        key = jnp.where(bits < 0, bits ^ jnp.int32(0x7FFFFFFF), bits + zero_keys)
        key = jnp.where(isc == 0.0, zero_keys - kpos, key)
        key_ref[t] = jnp.where(kpos <= qpos, key, INT_MIN)
        return carry

    lax.fori_loop(0, n_need, score_tile, 0)

    @pl.when(n_need % 2 == 1)
    def _():
        key_ref[n_need] = jnp.full((tq, tk), INT_MIN, jnp.int32)

    lane_rb = lax.broadcasted_iota(jnp.int32, (COUNT_ROWS, LANES), 1)

    def count(pred, *row_args):
        accs = []
        for rb in range(tq // COUNT_ROWS):
            rs = slice(rb * COUNT_ROWS, (rb + 1) * COUNT_ROWS)
            args = [a[rs] for a in row_args]

            def one(t, acc):
                for c in range(chunks):
                    kc = key_ref[t, rs, c * LANES:(c + 1) * LANES]
                    hit = pred(kc, t * tk + c * LANES + lane_rb, *args)
                    acc = acc + jnp.where(hit, 1.0, 0.0)
                return acc

            accs.append(lax.fori_loop(0, (n_need + 1) // 2, lambda u, a: one(2 * u + 1, one(2 * u, a)),
                                      jnp.zeros((COUNT_ROWS, LANES), F32)))
        total = jnp.sum(jnp.concatenate(accs, axis=0), axis=-1, keepdims=True)
        return jnp.broadcast_to(total, (tq, LANES))

    def thr_bit(it, carry):
        thr_u, n_at = carry
        cand_u = thr_u | lax.shift_left(jnp.int32(1), 31 - it)
        n_ge = count(lambda key, pos, c: key >= c, cand_u ^ jnp.int32(INT_MIN))
        take = n_ge >= DSA_TOPK
        return jnp.where(take, cand_u, thr_u), jnp.where(take, n_ge, n_at)

    all_keys = jnp.full((tq, LANES), 2.0 * DSA_TOPK, F32)
    thr_u, n_at = lax.fori_loop(0, 32, thr_bit, (jnp.zeros((tq, LANES), jnp.int32), all_keys))
    thr = thr_u ^ jnp.int32(INT_MIN)
    short = thr == INT_MIN
    excess = jnp.where(short, 0.0, n_at - DSA_TOPK)

    def tie_search():
        need = DSA_TOPK - count(lambda key, pos, c: key > c, thr)

        def pos_bit(it, last):
            cand = last | lax.shift_left(jnp.int32(1), seq_bits - 1 - it)
            n_before = count(lambda key, pos, c, th: (key == th) & (pos < c), cand, thr)
            return jnp.where(n_before < need, cand, last)

        return lax.fori_loop(0, seq_bits, pos_bit, jnp.zeros((tq, LANES), jnp.int32))

    everything = jnp.full((tq, LANES), 2 ** seq_bits, jnp.int32)
    last = lax.cond(jnp.max(excess) > 0.0, tie_search, lambda: everything)
    last = jnp.where(short, -1, last)
    thr_w, last_w = wide(thr), wide(last)

    def attend(j, carry):
        rows = pl.ds(pl.multiple_of(j * tk, tk), tk)
        key = key_ref[j]
        keep = (key > thr_w) | ((key == thr_w) & (j * tk + lane_pos <= last_w))
        bias_ref[...] = jnp.where(keep, 0.0, NEG)
        for p in range(WIDTH // LANES):
            sl = slice(p * LANES, (p + 1) * LANES)
            qh = _head_queries(q_ref[:, sl])
            kb = k_ref[rows, sl]
            vb = _with_ones(v_ref[rows, sl])
            for k in range(2):
                s = _dot_nt(qh[k], kb) + bias_ref[...]
                _online_softmax(s, vb, m_ref, l_ref, acc_ref, 2 * p + k)
        return carry

    lax.fori_loop(0, n_need, attend, 0)
    for p in range(WIDTH // LANES):
        o_ref[:, p * LANES:(p + 1) * LANES] = _pair_output(l_ref, acc_ref, p)


def _dsa_branch(ua, ub, bsz, seq):
    n = ub.shape[0]
    tq, tk = min(DSA_TQ, seq), min(DSA_TK, seq)
    nq, nk = seq // tq, seq // tk
    assert seq & (seq - 1) == 0 and DSA_TOPK <= seq // 4
    once = pl.Buffered(1)
    return pl.pallas_call(
        functools.partial(_dsa_kernel, tk=tk, seq_bits=seq.bit_length() - 1),
        grid=(bsz, nq),
        in_specs=[
            pl.BlockSpec((tq, WIDTH), lambda b, i: (b * nq + i, B_DQ * LANES // WIDTH)),
            pl.BlockSpec((seq, WIDTH), lambda b, i: (b, B_DK * LANES // WIDTH), pipeline_mode=once),
            pl.BlockSpec((seq, WIDTH), lambda b, i: (b, B_DV * LANES // WIDTH), pipeline_mode=once),
            pl.BlockSpec((tq, 2 * LANES), lambda b, i: (b * nq + i, B_IQ // 2)),
            pl.BlockSpec((tq, LANES), lambda b, i: (b * nq + i, A_IW)),
            pl.BlockSpec((seq, LANES), lambda b, i: (b, B_IK), pipeline_mode=once),
        ],
        out_specs=pl.BlockSpec((tq, WIDTH), lambda b, i: (b * nq + i, 0)),
        out_shape=jax.ShapeDtypeStruct((n, WIDTH), F32),
        scratch_shapes=[pltpu.VMEM((N_HEADS, tq, LANES), F32), pltpu.VMEM((N_HEADS, tq, LANES), F32),
                        pltpu.VMEM((N_HEADS, tq, LANES), F32),
                        pltpu.VMEM((nk + nk % 2, tq, tk), jnp.int32),
                        pltpu.VMEM((tq, tk), F32)],
        compiler_params=_params(("arbitrary",) * 2),
        name="dsa_attention",
    )(ub, ub, ub, ub, ua, ub)


def _merge_kernel(x_ref, gpre_ref, y0_ref, y1_ref, y2_ref, y3_ref, wg_ref, bg_ref, wb_ref, wo_ref,
                  gpost_ref, o_ref):
    x = x_ref[...]
    d = x.shape[1]
    hb = _rms(x, gpre_ref[...]).astype(BF16)
    merged = jnp.zeros(x.shape, F32)
    for n, y_ref in enumerate((y0_ref, y1_ref, y2_ref, y3_ref)):
        logit = _dot(hb, wg_ref[:, n * d:(n + 1) * d]) + bg_ref[:, n * d:(n + 1) * d]
        gate = 1.0 / (1.0 + jnp.exp(-logit))
        merged = merged + gate * _dot(y_ref[...].astype(BF16), wb_ref[n])
    mix = _dot(merged.astype(BF16), wo_ref[...])
    o_ref[...] = x + _rms(mix, gpost_ref[...])


def _merge(x2, g_pre, ys, w_gate, b_gate, w_branch, w_out, g_post):
    n, d = x2.shape
    tm = ROW_TILE
    const2 = lambda i: (0, 0)
    row = lambda i: (i, 0)
    return pl.pallas_call(
        _merge_kernel,
        grid=(n // tm,),
        in_specs=[pl.BlockSpec((tm, d), row), pl.BlockSpec((1, d), const2)]
        + [pl.BlockSpec((tm, WIDTH), row)] * N_BRANCH
        + [pl.BlockSpec((d, N_BRANCH * d), const2), pl.BlockSpec((1, N_BRANCH * d), const2),
           pl.BlockSpec((N_BRANCH, WIDTH, d), lambda i: (0, 0, 0)), pl.BlockSpec((d, d), const2),
           pl.BlockSpec((1, d), const2)],
        out_specs=pl.BlockSpec((tm, d), row),
        out_shape=jax.ShapeDtypeStruct((n, d), F32),
        compiler_params=_params(("arbitrary",)),
        name="branch_merge",
    )(x2, g_pre.reshape(1, d), *ys, w_gate.astype(BF16), b_gate.reshape(1, -1),
      w_branch.astype(BF16), w_out.astype(BF16), g_post.reshape(1, d))


FFN_CHUNK = 256


def _ffn_kernel(x_ref, gpre_ref, wi_ref, wo_ref, gpost_ref, o_ref):
    x = x_ref[...]
    hb = _rms(x, gpre_ref[...]).astype(BF16)
    acc = jnp.zeros(x.shape, F32)
    for c in range(FFN_HIDDEN // FFN_CHUNK):
        lo = c * FFN_CHUNK
        gate = _dot(hb, wi_ref[:, lo:lo + FFN_CHUNK])
        up = _dot(hb, wi_ref[:, FFN_HIDDEN + lo:FFN_HIDDEN + lo + FFN_CHUNK])
        acc = acc + _dot((_silu(gate) * up).astype(BF16), wo_ref[lo:lo + FFN_CHUNK, :])
    o_ref[...] = x + _rms(acc, gpost_ref[...])


def _ffn(x2, g_pre, w_ffn_in, w_ffn_out, g_post):
    n, d = x2.shape
    tm = ROW_TILE
    const2 = lambda i: (0, 0)
    row = lambda i: (i, 0)
    return pl.pallas_call(
        _ffn_kernel,
        grid=(n // tm,),
        in_specs=[pl.BlockSpec((tm, d), row), pl.BlockSpec((1, d), const2),
                  pl.BlockSpec((d, 2 * FFN_HIDDEN), const2), pl.BlockSpec((FFN_HIDDEN, d), const2),
                  pl.BlockSpec((1, d), const2)],
        out_specs=pl.BlockSpec((tm, d), row),
        out_shape=jax.ShapeDtypeStruct((n, d), F32),
        compiler_params=_params(("arbitrary",)),
        name="swiglu_ffn",
    )(x2, g_pre.reshape(1, d), w_ffn_in.astype(BF16), w_ffn_out.astype(BF16), g_post.reshape(1, d))


def kernel(x, norm_mix_pre, w_in, conv_w, conv_b, dt_bias, a_log, d_skip, ssm_norm, fox_fbias, w_gate, b_gate, w_branch, w_out, norm_mix_post, norm_ffn_pre, w_ffn_in, w_ffn_out, norm_ffn_post):
    bsz, seq, d = x.shape
    x2 = x.reshape(bsz * seq, d)
    cos, sin = _rope_tables(seq)
    for l in range(w_in.shape[0]):
        w, wt = _rearranged_w_in(w_in[l])
        ua, ub, t, km = _in_projection(x2, norm_mix_pre[l], w, wt, cos, sin, fox_fbias[l], seq)
        ys = (_ssd_branch(ua, t, conv_w[l], conv_b[l], dt_bias[l], a_log[l], d_skip[l], ssm_norm[l], bsz, seq),
              _fox_branch(ub, t, bsz, seq),
              _moba_branch(ub, km, bsz, seq),
              _dsa_branch(ua, ub, bsz, seq))
        x2 = _merge(x2, norm_mix_pre[l], ys, w_gate[l], b_gate[l], w_branch[l], w_out[l], norm_mix_post[l])
        x2 = _ffn(x2, norm_ffn_pre[l], w_ffn_in[l], w_ffn_out[l], norm_ffn_post[l])
    return x2.reshape(bsz, seq, d)
```

```python
import functools
import math

import jax
import jax.numpy as jnp
from jax import lax
from jax.experimental import pallas as pl
from jax.experimental.pallas import tpu as pltpu

F32 = jnp.float32
BF16 = jnp.bfloat16

D_MODEL = 1024
HEAD_DIM = 64
N_HEADS = 8
WIDTH = N_HEADS * HEAD_DIM
N_BRANCH = 4
SSM_GROUPS = 2
SSM_STATE = 128
SSM_CONV = 4
SSM_CHUNK = 256
MOBA_BLOCK = 256
MOBA_TOPK = 3
IDX_HEADS = 4
IDX_DIM = 64
DSA_TOPK = 256
ROPE_THETA = 10000.0
FFN_HIDDEN = -(-8 * D_MODEL // (3 * 256)) * 256
EPS = 1e-6

LANES = 128
NEG = -1e30
VMEM_LIMIT = 56 * 1024 * 1024

LOG2E = math.log2(math.e)

A_XBC, A_Z, A_DT, A_IW, NA = 0, 8, 12, 13, 14
B_DV, B_FQ, B_FK, B_FV, B_MV = 0, 4, 8, 12, 16
B_DQ, B_DK, B_MQ, B_MK, B_IQ, B_IK, NB = 20, 24, 28, 32, 36, 38, 40
B_ROPE = B_DQ
N_COLBLK = NA + NB
PROJ_TN = 256
ROW_TILE = 512
MLP_TILE = 512


def _dot(a, b):
    return jnp.dot(a, b, preferred_element_type=F32)


def _dot_nt(a, b):
    return lax.dot_general(a, b, (((1,), (1,)), ((), ())), preferred_element_type=F32)


def _dot_tn(a, b):
    return lax.dot_general(a, b, (((0,), (0,)), ((), ())), preferred_element_type=F32)


def _dot_f32(a, b):
    return jnp.dot(a, b, preferred_element_type=F32, precision=lax.Precision.HIGHEST)


def _rms(x, g):
    return x * lax.rsqrt(jnp.mean(x * x, axis=-1, keepdims=True) + EPS) * g


def _silu(x):
    return x * (1.0 / (1.0 + jnp.exp(-x)))


def _softplus(x):
    return jnp.maximum(x, 0.0) + jnp.log(1.0 + jnp.exp(-jnp.abs(x)))


def _params(sem):
    return pltpu.CompilerParams(dimension_semantics=sem, vmem_limit_bytes=VMEM_LIMIT)


def _inproj_kernel(x_ref, g_ref, w_ref, wt_ref, cos_ref, sin_ref, fb_ref, tri_ref,
                   ua_ref, ub_ref, t_ref, km_ref, carry_ref, *, tiles_per_seq):
    i = pl.program_id(0)
    hb = _rms(x_ref[...], g_ref[...]).astype(BF16)
    tm = hb.shape[0]
    lane = lax.broadcasted_iota(jnp.int32, (tm, PROJ_TN), 1)
    first_half = (lane % HEAD_DIM) < (HEAD_DIM // 2)
    n_tiles = N_COLBLK * LANES // PROJ_TN
    for j in range(n_tiles):
        c0 = j * PROJ_TN
        acc = _dot(hb, w_ref[:, c0:c0 + PROJ_TN])
        blk = c0 // LANES
        if blk < NA:
            ua_ref[:, c0:c0 + PROJ_TN] = acc
            continue
        blk -= NA
        if blk >= B_ROPE:
            partner = jnp.where(first_half,
                                pltpu.roll(acc, PROJ_TN - HEAD_DIM // 2, 1),
                                pltpu.roll(acc, HEAD_DIM // 2, 1))
            cos = jnp.concatenate([cos_ref[...]] * (PROJ_TN // LANES), axis=1)
            sin = jnp.concatenate([sin_ref[...]] * (PROJ_TN // LANES), axis=1)
            acc = acc * cos + partner * sin
            if B_MK <= blk < B_MK + WIDTH // LANES:
                o = (blk - B_MK) * LANES
                for r in range(tm // MOBA_BLOCK):
                    km_ref[r, :, o:o + PROJ_TN] = jnp.mean(acc[r * MOBA_BLOCK:(r + 1) * MOBA_BLOCK],
                                                           axis=0, keepdims=True)
        ub_ref[:, blk * LANES:blk * LANES + PROJ_TN] = acc.astype(BF16)

    t = _dot_nt(wt_ref[...], hb)

    @pl.when(i % tiles_per_seq == 0)
    def _():
        carry_ref[...] = jnp.zeros_like(carry_ref)

    logf = -_softplus(-(t[N_HEADS:] + fb_ref[...]))
    cf = _dot_f32(logf, tri_ref[...]) + carry_ref[...]
    carry_ref[...] = cf[:, tm - 1:tm]
    t_ref[0:N_HEADS, :] = t[:N_HEADS]
    t_ref[N_HEADS:, :] = cf * LOG2E


def _rearranged_w_in(w_in):
    o = 0
    parts = {}
    for name, wdt in (("z", 512), ("xs", 512), ("bs", 256), ("cs", 256), ("dt", 8),
                      ("fq", 512), ("fk", 512), ("fv", 512), ("ff", 8),
                      ("mq", 512), ("mk", 512), ("mv", 512),
                      ("dq", 512), ("dk", 512), ("dv", 512),
                      ("iq", 256), ("ik", 64), ("iw", 4)):
        parts[name] = w_in[:, o:o + wdt]
        o += wdt
    d = w_in.shape[0]
    pad = lambda n: jnp.zeros((d, n), w_in.dtype)
    qs = HEAD_DIM ** -0.5 * LOG2E
    cols = [parts["xs"], parts["bs"], parts["cs"], parts["z"],
            parts["dt"], pad(LANES - 8),
            parts["iw"] * (IDX_HEADS ** -0.5), pad(LANES - 4),
            parts["dv"], parts["fq"] * qs, parts["fk"], parts["fv"], parts["mv"],
            parts["dq"] * qs, parts["dk"], parts["mq"] * qs, parts["mk"],
            parts["iq"] * (IDX_DIM ** -0.5), parts["ik"], parts["ik"], pad(LANES)]
    w = jnp.concatenate(cols, axis=1).astype(BF16)
    assert w.shape[1] == N_COLBLK * LANES
    wt = jnp.concatenate([parts["dt"], parts["ff"]], axis=1).T.astype(BF16)
    return w, wt


def _rope_tables(seq):
    pos = jnp.arange(seq, dtype=F32)
    inv = ROPE_THETA ** (-jnp.arange(0, HEAD_DIM, 2, dtype=F32) / HEAD_DIM)
    ang = pos[:, None] * inv[None, :]
    c, s = jnp.cos(ang), jnp.sin(ang)
    cos = jnp.concatenate([c, c, c, c], axis=1)
    sin = jnp.concatenate([-s, s, -s, s], axis=1)
    return cos, sin


def _in_projection(x2, g, w, wt, cos, sin, fbias, seq):
    n, d = x2.shape
    tm = ROW_TILE
    tiles_per_seq = seq // tm
    ncol = N_COLBLK * LANES
    tri = jnp.triu(jnp.ones((tm, tm), F32))
    const = lambda i: (0, 0)
    return pl.pallas_call(
        functools.partial(_inproj_kernel, tiles_per_seq=tiles_per_seq),
        grid=(n // tm,),
        in_specs=[
            pl.BlockSpec((tm, d), lambda i: (i, 0)),
            pl.BlockSpec((1, d), const),
            pl.BlockSpec((d, ncol), const, pipeline_mode=pl.Buffered(1)),
            pl.BlockSpec((2 * N_HEADS, d), const),
            pl.BlockSpec((tm, LANES), lambda i: (i % tiles_per_seq, 0)),
            pl.BlockSpec((tm, LANES), lambda i: (i % tiles_per_seq, 0)),
            pl.BlockSpec((N_HEADS, 1), const),
            pl.BlockSpec((tm, tm), const),
        ],
        out_specs=[
            pl.BlockSpec((tm, NA * LANES), lambda i: (i, 0)),
            pl.BlockSpec((tm, NB * LANES), lambda i: (i, 0)),
            pl.BlockSpec((2 * N_HEADS, tm), lambda i: (0, i)),
            pl.BlockSpec((tm // MOBA_BLOCK, 1, WIDTH), lambda i: (i, 0, 0)),
        ],
        out_shape=[
            jax.ShapeDtypeStruct((n, NA * LANES), F32),
            jax.ShapeDtypeStruct((n, NB * LANES), BF16),
            jax.ShapeDtypeStruct((2 * N_HEADS, n), F32),
            jax.ShapeDtypeStruct((n // MOBA_BLOCK, 1, WIDTH), F32),
        ],
        scratch_shapes=[pltpu.VMEM((N_HEADS, 1), F32)],
        compiler_params=_params(("arbitrary",)),
        name="in_projection",
    )(x2, g.reshape(1, d), w, wt, cos, sin, fbias.reshape(N_HEADS, 1), tri)


def _ssd_kernel(z_ref, xbc_ref, dtc_ref, t_ref, cw_ref, cb_ref, dtb_row_ref, dtb_col_ref,
                alog_row_ref, alog_col_ref, dskip_ref, nw_ref, tril_ref, triu_ref, expand_ref,
                y_ref, ext_ref, state_ref):
    c = pl.program_id(1)
    q = SSM_CHUNK
    halo = 8
    n_pair = WIDTH // LANES

    @pl.when(c == 0)
    def _():
        ext_ref[0:halo, :] = jnp.zeros((halo, ext_ref.shape[1]), F32)
        state_ref[...] = jnp.zeros_like(state_ref)

    @pl.when(c > 0)
    def _():
        ext_ref[0:halo, :] = ext_ref[q:q + halo, :]

    ext_ref[halo:halo + q, :] = xbc_ref[...]
    conv = cb_ref[...]
    for w in range(SSM_CONV):
        o = halo - (SSM_CONV - 1) + w
        conv = conv + ext_ref[o:o + q, :] * cw_ref[w:w + 1, :]
    xbc = _silu(conv)
    xs = xbc[:, :WIDTH]
    bmat = xbc[:, WIDTH:WIDTH + SSM_GROUPS * SSM_STATE].astype(BF16)
    cmat = xbc[:, WIDTH + SSM_GROUPS * SSM_STATE:].astype(BF16)

    dt_col = _softplus(dtc_ref[...] + dtb_row_ref[...])
    dt_row = _softplus(t_ref[0:N_HEADS, :] + dtb_col_ref[...])
    a_row = -jnp.exp(alog_row_ref[...])
    a_col = -jnp.exp(alog_col_ref[...])
    acs_col = _dot_f32(tril_ref[...], dt_col * a_row)
    acs_row = _dot_f32(dt_row * a_col, triu_ref[...])
    alast_row = acs_col[q - 1:q, :]
    alast_col = acs_row[:, q - 1:q]

    expand = expand_ref[...]
    xdt = xs * _dot_f32(dt_col, expand)
    dstate_x = _dot_f32(jnp.exp(alast_row - acs_col), expand)
    ea_x = _dot_f32(jnp.exp(acs_col), expand)

    li = lax.broadcasted_iota(jnp.int32, (q, q), 0)
    si = lax.broadcasted_iota(jnp.int32, (q, q), 1)
    causal = li >= si
    lane = lax.broadcasted_iota(jnp.int32, (q, LANES), 1)
    sub = lax.broadcasted_iota(jnp.int32, (LANES, 1), 0)

    cb = []
    for g in range(SSM_GROUPS):
        cg = cmat[:, g * SSM_STATE:(g + 1) * SSM_STATE]
        bg = bmat[:, g * SSM_STATE:(g + 1) * SSM_STATE]
        cb.append(_dot_nt(cg, bg))

    for p in range(n_pair):
        g = (2 * p) // (N_HEADS // SSM_GROUPS)
        cg = cmat[:, g * SSM_STATE:(g + 1) * SSM_STATE]
        bg = bmat[:, g * SSM_STATE:(g + 1) * SSM_STATE]
        sl = slice(p * LANES, (p + 1) * LANES)
        xdt_p = xdt[:, sl]
        y = jnp.zeros((q, LANES), F32)
        for k in range(2):
            h = 2 * p + k
            seg = acs_col[:, h:h + 1] - acs_row[h:h + 1, :]
            decay = jnp.exp(jnp.where(causal, seg, NEG))
            scores = (cb[g] * decay).astype(BF16)
            in_head = (lane >= k * HEAD_DIM) & (lane < (k + 1) * HEAD_DIM)
            y = y + _dot(scores, jnp.where(in_head, xdt_p, 0.0).astype(BF16))
        prev = state_ref[p]
        y = y + _dot_nt(cg, prev.astype(BF16)) * ea_x[:, sl]
        contrib = _dot_tn((xdt_p * dstate_x[:, sl]).astype(BF16), bg)
        cd = jnp.exp(jnp.where(sub < HEAD_DIM, alast_col[2 * p:2 * p + 1, :],
                               alast_col[2 * p + 1:2 * p + 2, :]))
        state_ref[p] = prev * cd + contrib
        y = y + dskip_ref[:, sl] * xs[:, sl]
        y_ref[:, sl] = y * _silu(z_ref[:, sl])

    gw = WIDTH // SSM_GROUPS
    for g in range(SSM_GROUPS):
        yg = y_ref[:, g * gw:(g + 1) * gw]
        y_ref[:, g * gw:(g + 1) * gw] = _rms(yg, nw_ref[:, g * gw:(g + 1) * gw])


def _ssd_branch(u, t, conv_w, conv_b, dt_bias, a_log, d_skip, ssm_norm, bsz, seq):
    n = u.shape[0]
    q = SSM_CHUNK
    nc = seq // q
    cch = conv_w.shape[1]
    pad_row = lambda v: jnp.zeros((1, LANES), F32).at[0, :N_HEADS].set(v)
    expand = (jnp.arange(LANES)[:, None] == (jnp.arange(WIDTH)[None, :] // HEAD_DIM)).astype(F32)
    tril = jnp.tril(jnp.ones((q, q), F32))
    const = lambda b, c: (0, 0)
    row = lambda b, c: (b * nc + c, 0)
    return pl.pallas_call(
        _ssd_kernel,
        grid=(bsz, nc),
        in_specs=[
            pl.BlockSpec((q, WIDTH), lambda b, c: (b * nc + c, A_Z * LANES // WIDTH)),
            pl.BlockSpec((q, cch), lambda b, c: (b * nc + c, A_XBC * LANES // cch)),
            pl.BlockSpec((q, LANES), lambda b, c: (b * nc + c, A_DT)),
            pl.BlockSpec((2 * N_HEADS, q), lambda b, c: (0, b * nc + c)),
            pl.BlockSpec((SSM_CONV, cch), const),
            pl.BlockSpec((1, cch), const),
            pl.BlockSpec((1, LANES), const),
            pl.BlockSpec((N_HEADS, 1), const),
            pl.BlockSpec((1, LANES), const),
            pl.BlockSpec((N_HEADS, 1), const),
            pl.BlockSpec((1, WIDTH), const),
            pl.BlockSpec((1, WIDTH), const),
            pl.BlockSpec((q, q), const),
            pl.BlockSpec((q, q), const),
            pl.BlockSpec((LANES, WIDTH), const),
        ],
        out_specs=pl.BlockSpec((q, WIDTH), row),
        out_shape=jax.ShapeDtypeStruct((n, WIDTH), F32),
        scratch_shapes=[pltpu.VMEM((q + 8, cch), F32),
                        pltpu.VMEM((WIDTH // LANES, LANES, SSM_STATE), F32)],
        compiler_params=_params(("arbitrary", "arbitrary")),
        name="ssd_branch",
    )(u, u, u, t, conv_w, conv_b.reshape(1, cch), pad_row(dt_bias), dt_bias.reshape(N_HEADS, 1),
      pad_row(a_log), a_log.reshape(N_HEADS, 1), jnp.repeat(d_skip, HEAD_DIM).reshape(1, WIDTH),
      ssm_norm.reshape(1, WIDTH), tril, tril.T, expand)


ATT_TQ = 512
ATT_TK = 512


def _head_lanes(shape, k):
    lane = lax.broadcasted_iota(jnp.int32, shape, len(shape) - 1)
    return (lane >= k * HEAD_DIM) & (lane < (k + 1) * HEAD_DIM)


def _head_queries(q):
    return [jnp.where(_head_lanes(q.shape, k), q, jnp.zeros_like(q)) for k in range(2)]


def _head_values(v):
    return [jnp.where(_head_lanes(v.shape, k), v, jnp.ones_like(v)) for k in range(2)]


def _online_softmax(s, v_ext, m_ref, acc_ref, h):
    reps = s.shape[1] // LANES
    m_old = m_ref[h]
    m_new = jnp.maximum(m_old, jnp.max(s, axis=-1, keepdims=True))
    alpha = jnp.exp2(m_old - m_new)
    p = jnp.exp2((s - jnp.concatenate([m_new] * reps, axis=1)).astype(BF16))
    acc_ref[h] = alpha * acc_ref[h] + _dot(p, v_ext)
    m_ref[h] = m_new


def _init_softmax(m_ref, acc_ref):
    m_ref[...] = jnp.full(m_ref.shape, NEG, F32)
    acc_ref[...] = jnp.zeros(acc_ref.shape, F32)


def _pair_output(acc_ref, p):
    outs = [acc_ref[2 * p + k] / pltpu.roll(acc_ref[2 * p + k], HEAD_DIM, 1) for k in range(2)]
    return jnp.where(_head_lanes(outs[0].shape, 0), outs[0], outs[1])


def _fox_kernel(q_ref, k_ref, v_ref, cf0_ref, cf1_ref, o_ref, m_ref, acc_ref, *, tk):
    qi = pl.program_id(2)
    tq = q_ref.shape[0]
    _init_softmax(m_ref, acc_ref)
    qh = _head_queries(q_ref[...])

    def tile(j, diagonal):
        rows = pl.ds(pl.multiple_of(j * tk, tk), tk)
        kb = k_ref[rows, :]
        vb = _head_values(v_ref[rows, :])
        for k, cf_ref in enumerate((cf0_ref, cf1_ref)):
            s = _dot_nt(qh[k], kb) - cf_ref[0, j]
            if diagonal:
                row = lax.broadcasted_iota(jnp.int32, (tq, tk), 0)
                col = lax.broadcasted_iota(jnp.int32, (tq, tk), 1)
                s = jnp.where(col <= row, s, NEG)
            _online_softmax(s, vb[k], m_ref, acc_ref, k)

    def body(u, carry):
        tile(2 * u, False)
        tile(2 * u + 1, False)
        return carry

    lax.fori_loop(0, qi // 2, body, 0)
    pl.when(qi % 2 == 1)(lambda: tile(qi - 1, False))
    tile(qi, True)
    o_ref[...] = _pair_output(acc_ref, 0)


def _fox_branch(ub, t, bsz, seq):
    n = ub.shape[0]
    tq = tk = min(ATT_TQ, seq)
    nq, nk = seq // tq, seq // tk
    t4 = t.reshape(2 * N_HEADS, bsz * nk, 1, tk)
    kv = lambda col: (lambda b, p, i: (b, col + p))
    cf = lambda k: (lambda b, p, i: (N_HEADS + 2 * p + k, b, 0, 0))
    return pl.pallas_call(
        functools.partial(_fox_kernel, tk=tk),
        grid=(bsz, WIDTH // LANES, nq),
        in_specs=[
            pl.BlockSpec((tq, LANES), lambda b, p, i: (b * nq + i, B_FQ + p)),
            pl.BlockSpec((seq, LANES), kv(B_FK)),
            pl.BlockSpec((seq, LANES), kv(B_FV)),
            pl.BlockSpec((1, nk, 1, tk), cf(0)),
            pl.BlockSpec((1, nk, 1, tk), cf(1)),
        ],
        out_specs=pl.BlockSpec((tq, LANES), lambda b, p, i: (b * nq + i, p)),
        out_shape=jax.ShapeDtypeStruct((n, WIDTH), F32),
        scratch_shapes=[pltpu.VMEM((2, tq, LANES), F32), pltpu.VMEM((2, tq, LANES), F32)],
        compiler_params=_params(("arbitrary",) * 3),
        name="fox_attention",
    )(ub, ub, ub, t4, t4)


BLOCK_MASK = -30000.0


def _moba_kernel(q_ref, k_ref, v_ref, km_ref, o_ref, m_ref, acc_ref, *, tk, n_slots):
    qi = pl.program_id(2)
    tq = q_ref.shape[0]
    blocks_per_tile = tq // MOBA_BLOCK
    _init_softmax(m_ref, acc_ref)
    q = q_ref[...]
    qh = _head_queries(q)
    blk = lax.broadcasted_iota(jnp.int32, (n_slots, tq), 0)
    cur = qi * blocks_per_tile + lax.broadcasted_iota(jnp.int32, (n_slots, tq), 1) // MOBA_BLOCK
    kmb = km_ref[0, 0:n_slots, :].astype(BF16)
    lane_k = lax.broadcasted_iota(jnp.int32, (tk, LANES), 1)
    blk_k = lax.broadcasted_iota(jnp.int32, (tk, LANES), 0) // MOBA_BLOCK

    mask_base = (HEAD_DIM, 0)
    qm = []
    for k in range(2):
        gate = jnp.where(blk < cur, _dot_nt(kmb, qh[k]), -jnp.inf)
        picked = blk == cur
        for _ in range(MOBA_TOPK):
            mx = jnp.max(gate, axis=0, keepdims=True)
            idx = jnp.min(jnp.where(gate == mx, blk, n_slots), axis=0, keepdims=True)
            hit = blk == idx
            picked = picked | (hit & (blk < cur))
            gate = jnp.where(hit, -jnp.inf, gate)
        bias_t = jnp.where(picked, 0.0, BLOCK_MASK)
        bias_t = jnp.concatenate([bias_t, jnp.zeros((LANES - n_slots, tq), F32)], axis=0)
        bias = bias_t.T
        if mask_base[k]:
            bias = pltpu.roll(bias, mask_base[k], 1)
        qm.append(jnp.where(_head_lanes(q.shape, k), q, bias.astype(BF16)))

    def tile(j, diagonal):
        rows = pl.ds(pl.multiple_of(j * tk, tk), tk)
        kb = k_ref[rows, :]
        vb = _head_values(v_ref[rows, :])
        for k in range(2):
            onehot = jnp.where(lane_k == mask_base[k] + j * (tk // MOBA_BLOCK) + blk_k, 1.0, 0.0).astype(BF16)
            s = _dot_nt(qm[k], jnp.where(_head_lanes(kb.shape, k), kb, onehot))
            if diagonal:
                r2 = lax.broadcasted_iota(jnp.int32, (tq, tk), 0)
                c2 = lax.broadcasted_iota(jnp.int32, (tq, tk), 1)
                s = jnp.where((r2 // MOBA_BLOCK == c2 // MOBA_BLOCK) & (c2 > r2), NEG, s)
            _online_softmax(s, vb[k], m_ref, acc_ref, k)

    def body(u, carry):
        tile(2 * u, False)
        tile(2 * u + 1, False)
        return carry

    lax.fori_loop(0, qi // 2, body, 0)
    pl.when(qi % 2 == 1)(lambda: tile(qi - 1, False))
    tile(qi, True)
    o_ref[...] = _pair_output(acc_ref, 0)


def _moba_branch(ub, km, bsz, seq):
    n = ub.shape[0]
    tq = tk = min(ATT_TQ, seq)
    nq, nblk = seq // tq, seq // MOBA_BLOCK
    assert nblk <= HEAD_DIM and MOBA_TOPK == 3 and tk % MOBA_BLOCK == 0
    kmp = jnp.zeros((bsz, LANES, WIDTH), F32).at[:, :nblk].set(km.reshape(bsz, nblk, WIDTH))
    kv = lambda col: (lambda b, p, i: (b, col + p))
    return pl.pallas_call(
        functools.partial(_moba_kernel, tk=tk, n_slots=-(-nblk // 16) * 16),
        grid=(bsz, WIDTH // LANES, nq),
        in_specs=[
            pl.BlockSpec((tq, LANES), lambda b, p, i: (b * nq + i, B_MQ + p)),
            pl.BlockSpec((seq, LANES), kv(B_MK)),
            pl.BlockSpec((seq, LANES), kv(B_MV)),
            pl.BlockSpec((1, LANES, LANES), lambda b, p, i: (b, 0, p)),
        ],
        out_specs=pl.BlockSpec((tq, LANES), lambda b, p, i: (b * nq + i, p)),
        out_shape=jax.ShapeDtypeStruct((n, WIDTH), F32),
        scratch_shapes=[pltpu.VMEM((2, tq, LANES), F32), pltpu.VMEM((2, tq, LANES), F32)],
        compiler_params=_params(("arbitrary",) * 3),
        name="moba_attention",
    )(ub, ub, ub, kmp)


DSA_TQ = 512
DSA_TK = 512
COUNT_ROWS = 128
INT_MIN = -2 ** 31


def _dsa_kernel(q_ref, k_ref, v_ref, iq_ref, iw_ref, ik_ref, o_ref,
                m_ref, acc_ref, key_ref, bias_ref, *, tk, seq_bits):
    qi = pl.program_id(1)
    tq = q_ref.shape[0]
    chunks = tk // LANES
    n_need = (qi * tq + tq - 1) // tk + 1
    qpos = qi * tq + lax.broadcasted_iota(jnp.int32, (tq, tk), 0)
    lane_pos = lax.broadcasted_iota(jnp.int32, (tq, tk), 1)
    wide = lambda x: jnp.concatenate([x] * chunks, axis=1)

    _init_softmax(m_ref, acc_ref)
    iq = iq_ref[...]
    iw = iw_ref[...]
    iqh = [_head_queries(iq[:, b * LANES:(b + 1) * LANES]) for b in range(IDX_HEADS // 2)]

    zero_keys = jnp.int32(2 ** seq_bits)

    def score_tile(t, carry):
        kib = ik_ref[pl.ds(pl.multiple_of(t * tk, tk), tk), :]
        isc = jnp.zeros((tq, tk), F32)
        for h in range(IDX_HEADS):
            isc = isc + jnp.maximum(_dot_nt(iqh[h // 2][h % 2], kib), 0.0) * iw[:, h:h + 1]
        kpos = t * tk + lane_pos
        bits = pltpu.bitcast(isc, jnp.int32)
        key = jnp.where(bits < 0, bits ^ jnp.int32(0x7FFFFFFF), bits + zero_keys)
        key = jnp.where(isc == 0.0, zero_keys - kpos, key)
        key_ref[t] = jnp.where(kpos <= qpos, key, INT_MIN)
        return carry

    lax.fori_loop(0, n_need, score_tile, 0)

    @pl.when(n_need % 2 == 1)
    def _():
        key_ref[n_need] = jnp.full((tq, tk), INT_MIN, jnp.int32)

    lane_rb = lax.broadcasted_iota(jnp.int32, (COUNT_ROWS, LANES), 1)

    def count(pred, *row_args):
        accs = []
        for rb in range(tq // COUNT_ROWS):
            rs = slice(rb * COUNT_ROWS, (rb + 1) * COUNT_ROWS)
            args = [a[rs] for a in row_args]

            def one(t, acc):
                for c in range(chunks):
                    kc = key_ref[t, rs, c * LANES:(c + 1) * LANES]
                    hit = pred(kc, t * tk + c * LANES + lane_rb, *args)
                    acc = acc + jnp.where(hit, 1.0, 0.0)
                return acc

            accs.append(lax.fori_loop(0, (n_need + 1) // 2, lambda u, a: one(2 * u + 1, one(2 * u, a)),
                                      jnp.zeros((COUNT_ROWS, LANES), F32)))
        total = jnp.sum(jnp.concatenate(accs, axis=0), axis=-1, keepdims=True)
        return jnp.broadcast_to(total, (tq, LANES))

    def thr_bit(it, carry):
        thr_u, n_at = carry
        cand_u = thr_u | lax.shift_left(jnp.int32(1), 31 - it)
        n_ge = count(lambda key, pos, c: key >= c, cand_u ^ jnp.int32(INT_MIN))
        take = n_ge >= DSA_TOPK
        return jnp.where(take, cand_u, thr_u), jnp.where(take, n_ge, n_at)

    all_keys = jnp.full((tq, LANES), 2.0 * DSA_TOPK, F32)
    thr_u, n_at = lax.fori_loop(0, 32, thr_bit, (jnp.zeros((tq, LANES), jnp.int32), all_keys))
    thr = thr_u ^ jnp.int32(INT_MIN)
    short = thr == INT_MIN
    excess = jnp.where(short, 0.0, n_at - DSA_TOPK)

    def tie_search():
        need = DSA_TOPK - count(lambda key, pos, c: key > c, thr)

        def pos_bit(it, last):
            cand = last | lax.shift_left(jnp.int32(1), seq_bits - 1 - it)
            n_before = count(lambda key, pos, c, th: (key == th) & (pos < c), cand, thr)
            return jnp.where(n_before < need, cand, last)

        return lax.fori_loop(0, seq_bits, pos_bit, jnp.zeros((tq, LANES), jnp.int32))

    everything = jnp.full((tq, LANES), 2 ** seq_bits, jnp.int32)
    last = lax.cond(jnp.max(excess) > 0.0, tie_search, lambda: everything)
    last = jnp.where(short, -1, last)
    thr_w, last_w = wide(thr), wide(last)

    def attend(j, carry):
        rows = pl.ds(pl.multiple_of(j * tk, tk), tk)
        key = key_ref[j]
        keep = (key > thr_w) | ((key == thr_w) & (j * tk + lane_pos <= last_w))
        bias_ref[...] = jnp.where(keep, 0.0, NEG)
        for p in range(WIDTH // LANES):
            sl = slice(p * LANES, (p + 1) * LANES)
            qh = _head_queries(q_ref[:, sl])
            kb = k_ref[rows, sl]
            vb = _head_values(v_ref[rows, sl])
            for k in range(2):
                s = _dot_nt(qh[k], kb) + bias_ref[...]
                _online_softmax(s, vb[k], m_ref, acc_ref, 2 * p + k)
        return carry

    lax.fori_loop(0, n_need, attend, 0)
    for p in range(WIDTH // LANES):
        o_ref[:, p * LANES:(p + 1) * LANES] = _pair_output(acc_ref, p)


def _dsa_branch(ua, ub, bsz, seq):
    n = ub.shape[0]
    tq, tk = min(DSA_TQ, seq), min(DSA_TK, seq)
    nq, nk = seq // tq, seq // tk
    assert seq & (seq - 1) == 0 and DSA_TOPK <= seq // 4
    once = pl.Buffered(1)
    return pl.pallas_call(
        functools.partial(_dsa_kernel, tk=tk, seq_bits=seq.bit_length() - 1),
        grid=(bsz, nq),
        in_specs=[
            pl.BlockSpec((tq, WIDTH), lambda b, i: (b * nq + i, B_DQ * LANES // WIDTH)),
            pl.BlockSpec((seq, WIDTH), lambda b, i: (b, B_DK * LANES // WIDTH), pipeline_mode=once),
            pl.BlockSpec((seq, WIDTH), lambda b, i: (b, B_DV * LANES // WIDTH), pipeline_mode=once),
            pl.BlockSpec((tq, 2 * LANES), lambda b, i: (b * nq + i, B_IQ // 2)),
            pl.BlockSpec((tq, LANES), lambda b, i: (b * nq + i, A_IW)),
            pl.BlockSpec((seq, LANES), lambda b, i: (b, B_IK), pipeline_mode=once),
        ],
        out_specs=pl.BlockSpec((tq, WIDTH), lambda b, i: (b * nq + i, 0)),
        out_shape=jax.ShapeDtypeStruct((n, WIDTH), F32),
        scratch_shapes=[pltpu.VMEM((N_HEADS, tq, LANES), F32), pltpu.VMEM((N_HEADS, tq, LANES), F32),
                        pltpu.VMEM((nk + nk % 2, tq, tk), jnp.int32),
                        pltpu.VMEM((tq, tk), F32)],
        compiler_params=_params(("arbitrary",) * 2),
        name="dsa_attention",
    )(ub, ub, ub, ub, ua, ub)


def _merge_kernel(x_ref, gpre_ref, y0_ref, y1_ref, y2_ref, y3_ref, wg_ref, bg_ref, wb_ref, wo_ref,
                  gpost_ref, o_ref):
    x = x_ref[...]
    d = x.shape[1]
    hb = _rms(x, gpre_ref[...]).astype(BF16)
    merged = jnp.zeros(x.shape, F32)
    for n, y_ref in enumerate((y0_ref, y1_ref, y2_ref, y3_ref)):
        logit = _dot(hb, wg_ref[:, n * d:(n + 1) * d]) + bg_ref[:, n * d:(n + 1) * d]
        gate = 1.0 / (1.0 + jnp.exp(-logit))
        merged = merged + gate * _dot(y_ref[...].astype(BF16), wb_ref[n])
    mix = _dot(merged.astype(BF16), wo_ref[...])
    o_ref[...] = x + _rms(mix, gpost_ref[...])


def _merge(x2, g_pre, ys, w_gate, b_gate, w_branch, w_out, g_post):
    n, d = x2.shape
    tm = MLP_TILE
    const2 = lambda i: (0, 0)
    row = lambda i: (i, 0)
    once = pl.Buffered(1)
    return pl.pallas_call(
        _merge_kernel,
        grid=(n // tm,),
        in_specs=[pl.BlockSpec((tm, d), row), pl.BlockSpec((1, d), const2)]
        + [pl.BlockSpec((tm, WIDTH), row)] * N_BRANCH
        + [pl.BlockSpec((d, N_BRANCH * d), const2, pipeline_mode=once),
           pl.BlockSpec((1, N_BRANCH * d), const2),
           pl.BlockSpec((N_BRANCH, WIDTH, d), lambda i: (0, 0, 0), pipeline_mode=once),
           pl.BlockSpec((d, d), const2, pipeline_mode=once),
           pl.BlockSpec((1, d), const2)],
        out_specs=pl.BlockSpec((tm, d), row),
        out_shape=jax.ShapeDtypeStruct((n, d), F32),
        compiler_params=_params(("arbitrary",)),
        name="branch_merge",
    )(x2, g_pre.reshape(1, d), *ys, w_gate.astype(BF16), b_gate.reshape(1, -1),
      w_branch.astype(BF16), w_out.astype(BF16), g_post.reshape(1, d))


FFN_CHUNK = 256


def _ffn_kernel(x_ref, gpre_ref, wi_ref, wo_ref, gpost_ref, o_ref):
    x = x_ref[...]
    hb = _rms(x, gpre_ref[...]).astype(BF16)
    acc = jnp.zeros(x.shape, F32)
    for c in range(FFN_HIDDEN // FFN_CHUNK):
        lo = c * FFN_CHUNK
        gate = _dot(hb, wi_ref[:, lo:lo + FFN_CHUNK])
        up = _dot(hb, wi_ref[:, FFN_HIDDEN + lo:FFN_HIDDEN + lo + FFN_CHUNK])
        acc = acc + _dot((_silu(gate) * up).astype(BF16), wo_ref[lo:lo + FFN_CHUNK, :])
    o_ref[...] = x + _rms(acc, gpost_ref[...])


def _ffn(x2, g_pre, w_ffn_in, w_ffn_out, g_post):
    n, d = x2.shape
    tm = MLP_TILE
    const2 = lambda i: (0, 0)
    row = lambda i: (i, 0)
    once = pl.Buffered(1)
    return pl.pallas_call(
        _ffn_kernel,
        grid=(n // tm,),
        in_specs=[pl.BlockSpec((tm, d), row), pl.BlockSpec((1, d), const2),
                  pl.BlockSpec((d, 2 * FFN_HIDDEN), const2, pipeline_mode=once),
                  pl.BlockSpec((FFN_HIDDEN, d), const2, pipeline_mode=once),
                  pl.BlockSpec((1, d), const2)],
        out_specs=pl.BlockSpec((tm, d), row),
        out_shape=jax.ShapeDtypeStruct((n, d), F32),
        compiler_params=_params(("arbitrary",)),
        name="swiglu_ffn",
    )(x2, g_pre.reshape(1, d), w_ffn_in.astype(BF16), w_ffn_out.astype(BF16), g_post.reshape(1, d))


def kernel(x, norm_mix_pre, w_in, conv_w, conv_b, dt_bias, a_log, d_skip, ssm_norm, fox_fbias, w_gate, b_gate, w_branch, w_out, norm_mix_post, norm_ffn_pre, w_ffn_in, w_ffn_out, norm_ffn_post):
    bsz, seq, d = x.shape
    x2 = x.reshape(bsz * seq, d)
    cos, sin = _rope_tables(seq)
    for l in range(w_in.shape[0]):
        w, wt = _rearranged_w_in(w_in[l])
        ua, ub, t, km = _in_projection(x2, norm_mix_pre[l], w, wt, cos, sin, fox_fbias[l], seq)
        ys = (_ssd_branch(ua, t, conv_w[l], conv_b[l], dt_bias[l], a_log[l], d_skip[l], ssm_norm[l], bsz, seq),
              _fox_branch(ub, t, bsz, seq),
              _moba_branch(ub, km, bsz, seq),
              _dsa_branch(ua, ub, bsz, seq))
        x2 = _merge(x2, norm_mix_pre[l], ys, w_gate[l], b_gate[l], w_branch[l], w_out[l], norm_mix_post[l])
        x2 = _ffn(x2, norm_ffn_pre[l], w_ffn_in[l], w_ffn_out[l], norm_ffn_post[l])
    return x2.reshape(bsz, seq, d)
```

```python
import functools
import math

import jax
import jax.numpy as jnp
from jax import lax
from jax.experimental import pallas as pl
from jax.experimental.pallas import tpu as pltpu

F32 = jnp.float32
BF16 = jnp.bfloat16

D_MODEL = 1024
HEAD_DIM = 64
N_HEADS = 8
WIDTH = N_HEADS * HEAD_DIM
N_BRANCH = 4
SSM_GROUPS = 2
SSM_STATE = 128
SSM_CONV = 4
SSM_CHUNK = 256
MOBA_BLOCK = 256
MOBA_TOPK = 3
IDX_HEADS = 4
IDX_DIM = 64
DSA_TOPK = 256
ROPE_THETA = 10000.0
FFN_HIDDEN = -(-8 * D_MODEL // (3 * 256)) * 256
EPS = 1e-6

LANES = 128
NEG = -1e30
VMEM_LIMIT = 56 * 1024 * 1024

LOG2E = math.log2(math.e)

A_XBC, A_Z, A_DT, A_IW, NA = 0, 8, 12, 13, 14
B_DV, B_FQ, B_FK, B_FV, B_MV = 0, 4, 8, 12, 16
B_DQ, B_DK, B_MQ, B_MK, B_IQ, B_IK, NB = 20, 24, 28, 32, 36, 38, 40
B_ROPE = B_DQ
N_COLBLK = NA + NB
PROJ_TN = 256
ROW_TILE = 512
MLP_TILE = 512


def _dot(a, b):
    return jnp.dot(a, b, preferred_element_type=F32)


def _dot_nt(a, b):
    return lax.dot_general(a, b, (((1,), (1,)), ((), ())), preferred_element_type=F32)


def _dot_tn(a, b):
    return lax.dot_general(a, b, (((0,), (0,)), ((), ())), preferred_element_type=F32)


def _dot_f32(a, b):
    return jnp.dot(a, b, preferred_element_type=F32, precision=lax.Precision.HIGHEST)


def _rms(x, g):
    return x * lax.rsqrt(jnp.mean(x * x, axis=-1, keepdims=True) + EPS) * g


def _silu(x):
    return x * (1.0 / (1.0 + jnp.exp(-x)))


def _softplus(x):
    return jnp.maximum(x, 0.0) + jnp.log(1.0 + jnp.exp(-jnp.abs(x)))


def _params(sem):
    return pltpu.CompilerParams(dimension_semantics=sem, vmem_limit_bytes=VMEM_LIMIT)


def _inproj_kernel(x_ref, g_ref, w_ref, wt_ref, cos_ref, sin_ref, fb_ref, tri_ref,
                   ua_ref, ub_ref, t_ref, km_ref, carry_ref, *, tiles_per_seq):
    i = pl.program_id(0)
    hb = _rms(x_ref[...], g_ref[...]).astype(BF16)
    tm = hb.shape[0]
    lane = lax.broadcasted_iota(jnp.int32, (tm, PROJ_TN), 1)
    first_half = (lane % HEAD_DIM) < (HEAD_DIM // 2)
    n_tiles = N_COLBLK * LANES // PROJ_TN
    for j in range(n_tiles):
        c0 = j * PROJ_TN
        acc = _dot(hb, w_ref[:, c0:c0 + PROJ_TN])
        blk = c0 // LANES
        if blk < NA:
            ua_ref[:, c0:c0 + PROJ_TN] = acc
            continue
        blk -= NA
        if blk >= B_ROPE:
            partner = jnp.where(first_half,
                                pltpu.roll(acc, PROJ_TN - HEAD_DIM // 2, 1),
                                pltpu.roll(acc, HEAD_DIM // 2, 1))
            cos = jnp.concatenate([cos_ref[...]] * (PROJ_TN // LANES), axis=1)
            sin = jnp.concatenate([sin_ref[...]] * (PROJ_TN // LANES), axis=1)
            acc = acc * cos + partner * sin
            if B_MK <= blk < B_MK + WIDTH // LANES:
                o = (blk - B_MK) * LANES
                for r in range(tm // MOBA_BLOCK):
                    km_ref[r, :, o:o + PROJ_TN] = jnp.mean(acc[r * MOBA_BLOCK:(r + 1) * MOBA_BLOCK],
                                                           axis=0, keepdims=True)
        ub_ref[:, blk * LANES:blk * LANES + PROJ_TN] = acc.astype(BF16)

    t = _dot_nt(wt_ref[...], hb)

    @pl.when(i % tiles_per_seq == 0)
    def _():
        carry_ref[...] = jnp.zeros_like(carry_ref)

    logf = -_softplus(-(t[N_HEADS:] + fb_ref[...]))
    cf = _dot_f32(logf, tri_ref[...]) + carry_ref[...]
    carry_ref[...] = cf[:, tm - 1:tm]
    t_ref[0:N_HEADS, :] = t[:N_HEADS]
    t_ref[N_HEADS:, :] = cf * LOG2E


def _rearranged_w_in(w_in):
    o = 0
    parts = {}
    for name, wdt in (("z", 512), ("xs", 512), ("bs", 256), ("cs", 256), ("dt", 8),
                      ("fq", 512), ("fk", 512), ("fv", 512), ("ff", 8),
                      ("mq", 512), ("mk", 512), ("mv", 512),
                      ("dq", 512), ("dk", 512), ("dv", 512),
                      ("iq", 256), ("ik", 64), ("iw", 4)):
        parts[name] = w_in[:, o:o + wdt]
        o += wdt
    d = w_in.shape[0]
    pad = lambda n: jnp.zeros((d, n), w_in.dtype)
    qs = HEAD_DIM ** -0.5 * LOG2E
    cols = [parts["xs"], parts["bs"], parts["cs"], parts["z"],
            parts["dt"], pad(LANES - 8),
            parts["iw"] * (IDX_HEADS ** -0.5), pad(LANES - 4),
            parts["dv"], parts["fq"] * qs, parts["fk"], parts["fv"], parts["mv"],
            parts["dq"] * qs, parts["dk"], parts["mq"] * qs, parts["mk"],
            parts["iq"] * (IDX_DIM ** -0.5), parts["ik"], parts["ik"], pad(LANES)]
    w = jnp.concatenate(cols, axis=1).astype(BF16)
    assert w.shape[1] == N_COLBLK * LANES
    wt = jnp.concatenate([parts["dt"], parts["ff"]], axis=1).T.astype(BF16)
    return w, wt


def _rope_tables(seq):
    pos = jnp.arange(seq, dtype=F32)
    inv = ROPE_THETA ** (-jnp.arange(0, HEAD_DIM, 2, dtype=F32) / HEAD_DIM)
    ang = pos[:, None] * inv[None, :]
    c, s = jnp.cos(ang), jnp.sin(ang)
    cos = jnp.concatenate([c, c, c, c], axis=1)
    sin = jnp.concatenate([-s, s, -s, s], axis=1)
    return cos, sin


def _in_projection(x2, g, w, wt, cos, sin, fbias, seq):
    n, d = x2.shape
    tm = ROW_TILE
    tiles_per_seq = seq // tm
    ncol = N_COLBLK * LANES
    tri = jnp.triu(jnp.ones((tm, tm), F32))
    const = lambda i: (0, 0)
    return pl.pallas_call(
        functools.partial(_inproj_kernel, tiles_per_seq=tiles_per_seq),
        grid=(n // tm,),
        in_specs=[
            pl.BlockSpec((tm, d), lambda i: (i, 0)),
            pl.BlockSpec((1, d), const),
            pl.BlockSpec((d, ncol), const, pipeline_mode=pl.Buffered(1)),
            pl.BlockSpec((2 * N_HEADS, d), const),
            pl.BlockSpec((tm, LANES), lambda i: (i % tiles_per_seq, 0)),
            pl.BlockSpec((tm, LANES), lambda i: (i % tiles_per_seq, 0)),
            pl.BlockSpec((N_HEADS, 1), const),
            pl.BlockSpec((tm, tm), const),
        ],
        out_specs=[
            pl.BlockSpec((tm, NA * LANES), lambda i: (i, 0)),
            pl.BlockSpec((tm, NB * LANES), lambda i: (i, 0)),
            pl.BlockSpec((2 * N_HEADS, tm), lambda i: (0, i)),
            pl.BlockSpec((tm // MOBA_BLOCK, 1, WIDTH), lambda i: (i, 0, 0)),
        ],
        out_shape=[
            jax.ShapeDtypeStruct((n, NA * LANES), F32),
            jax.ShapeDtypeStruct((n, NB * LANES), BF16),
            jax.ShapeDtypeStruct((2 * N_HEADS, n), F32),
            jax.ShapeDtypeStruct((n // MOBA_BLOCK, 1, WIDTH), F32),
        ],
        scratch_shapes=[pltpu.VMEM((N_HEADS, 1), F32)],
        compiler_params=_params(("arbitrary",)),
        name="in_projection",
    )(x2, g.reshape(1, d), w, wt, cos, sin, fbias.reshape(N_HEADS, 1), tri)


def _ssd_kernel(z_ref, xbc_ref, dtc_ref, t_ref, cw_ref, cb_ref, dtb_row_ref, dtb_col_ref,
                alog_row_ref, alog_col_ref, dskip_ref, nw_ref, tril_ref, triu_ref, expand_ref,
                y_ref, ext_ref, state_ref):
    c = pl.program_id(1)
    q = SSM_CHUNK
    halo = 8
    n_pair = WIDTH // LANES

    @pl.when(c == 0)
    def _():
        ext_ref[0:halo, :] = jnp.zeros((halo, ext_ref.shape[1]), F32)
        state_ref[...] = jnp.zeros_like(state_ref)

    @pl.when(c > 0)
    def _():
        ext_ref[0:halo, :] = ext_ref[q:q + halo, :]

    ext_ref[halo:halo + q, :] = xbc_ref[...]
    conv = cb_ref[...]
    for w in range(SSM_CONV):
        o = halo - (SSM_CONV - 1) + w
        conv = conv + ext_ref[o:o + q, :] * cw_ref[w:w + 1, :]
    xbc = _silu(conv)
    xs = xbc[:, :WIDTH]
    bmat = xbc[:, WIDTH:WIDTH + SSM_GROUPS * SSM_STATE].astype(BF16)
    cmat = xbc[:, WIDTH + SSM_GROUPS * SSM_STATE:].astype(BF16)

    dt_col = _softplus(dtc_ref[...] + dtb_row_ref[...])
    dt_row = _softplus(t_ref[0:N_HEADS, :] + dtb_col_ref[...])
    a_row = -jnp.exp(alog_row_ref[...])
    a_col = -jnp.exp(alog_col_ref[...])
    acs_col = _dot_f32(tril_ref[...], dt_col * a_row)
    acs_row = _dot_f32(dt_row * a_col, triu_ref[...])
    alast_row = acs_col[q - 1:q, :]
    alast_col = acs_row[:, q - 1:q]

    expand = expand_ref[...]
    xdt = xs * _dot_f32(dt_col, expand)
    dstate_x = _dot_f32(jnp.exp(alast_row - acs_col), expand)
    ea_x = _dot_f32(jnp.exp(acs_col), expand)

    li = lax.broadcasted_iota(jnp.int32, (q, q), 0)
    si = lax.broadcasted_iota(jnp.int32, (q, q), 1)
    causal = li >= si
    lane = lax.broadcasted_iota(jnp.int32, (q, LANES), 1)
    sub = lax.broadcasted_iota(jnp.int32, (LANES, 1), 0)

    cb = []
    for g in range(SSM_GROUPS):
        cg = cmat[:, g * SSM_STATE:(g + 1) * SSM_STATE]
        bg = bmat[:, g * SSM_STATE:(g + 1) * SSM_STATE]
        cb.append(_dot_nt(cg, bg))

    for p in range(n_pair):
        g = (2 * p) // (N_HEADS // SSM_GROUPS)
        cg = cmat[:, g * SSM_STATE:(g + 1) * SSM_STATE]
        bg = bmat[:, g * SSM_STATE:(g + 1) * SSM_STATE]
        sl = slice(p * LANES, (p + 1) * LANES)
        xdt_p = xdt[:, sl]
        y = jnp.zeros((q, LANES), F32)
        for k in range(2):
            h = 2 * p + k
            seg = acs_col[:, h:h + 1] - acs_row[h:h + 1, :]
            decay = jnp.exp(jnp.where(causal, seg, NEG))
            scores = (cb[g] * decay).astype(BF16)
            in_head = (lane >= k * HEAD_DIM) & (lane < (k + 1) * HEAD_DIM)
            y = y + _dot(scores, jnp.where(in_head, xdt_p, 0.0).astype(BF16))
        prev = state_ref[p]
        y = y + _dot_nt(cg, prev.astype(BF16)) * ea_x[:, sl]
        contrib = _dot_tn((xdt_p * dstate_x[:, sl]).astype(BF16), bg)
        cd = jnp.exp(jnp.where(sub < HEAD_DIM, alast_col[2 * p:2 * p + 1, :],
                               alast_col[2 * p + 1:2 * p + 2, :]))
        state_ref[p] = prev * cd + contrib
        y = y + dskip_ref[:, sl] * xs[:, sl]
        y_ref[:, sl] = y * _silu(z_ref[:, sl])

    gw = WIDTH // SSM_GROUPS
    for g in range(SSM_GROUPS):
        yg = y_ref[:, g * gw:(g + 1) * gw]
        y_ref[:, g * gw:(g + 1) * gw] = _rms(yg, nw_ref[:, g * gw:(g + 1) * gw])


def _ssd_branch(u, t, conv_w, conv_b, dt_bias, a_log, d_skip, ssm_norm, bsz, seq):
    n = u.shape[0]
    q = SSM_CHUNK
    nc = seq // q
    cch = conv_w.shape[1]
    pad_row = lambda v: jnp.zeros((1, LANES), F32).at[0, :N_HEADS].set(v)
    expand = (jnp.arange(LANES)[:, None] == (jnp.arange(WIDTH)[None, :] // HEAD_DIM)).astype(F32)
    tril = jnp.tril(jnp.ones((q, q), F32))
    const = lambda b, c: (0, 0)
    row = lambda b, c: (b * nc + c, 0)
    return pl.pallas_call(
        _ssd_kernel,
        grid=(bsz, nc),
        in_specs=[
            pl.BlockSpec((q, WIDTH), lambda b, c: (b * nc + c, A_Z * LANES // WIDTH)),
            pl.BlockSpec((q, cch), lambda b, c: (b * nc + c, A_XBC * LANES // cch)),
            pl.BlockSpec((q, LANES), lambda b, c: (b * nc + c, A_DT)),
            pl.BlockSpec((2 * N_HEADS, q), lambda b, c: (0, b * nc + c)),
            pl.BlockSpec((SSM_CONV, cch), const),
            pl.BlockSpec((1, cch), const),
            pl.BlockSpec((1, LANES), const),
            pl.BlockSpec((N_HEADS, 1), const),
            pl.BlockSpec((1, LANES), const),
            pl.BlockSpec((N_HEADS, 1), const),
            pl.BlockSpec((1, WIDTH), const),
            pl.BlockSpec((1, WIDTH), const),
            pl.BlockSpec((q, q), const),
            pl.BlockSpec((q, q), const),
            pl.BlockSpec((LANES, WIDTH), const),
        ],
        out_specs=pl.BlockSpec((q, WIDTH), row),
        out_shape=jax.ShapeDtypeStruct((n, WIDTH), F32),
        scratch_shapes=[pltpu.VMEM((q + 8, cch), F32),
                        pltpu.VMEM((WIDTH // LANES, LANES, SSM_STATE), F32)],
        compiler_params=_params(("arbitrary", "arbitrary")),
        name="ssd_branch",
    )(u, u, u, t, conv_w, conv_b.reshape(1, cch), pad_row(dt_bias), dt_bias.reshape(N_HEADS, 1),
      pad_row(a_log), a_log.reshape(N_HEADS, 1), jnp.repeat(d_skip, HEAD_DIM).reshape(1, WIDTH),
      ssm_norm.reshape(1, WIDTH), tril, tril.T, expand)


ATT_TQ = 1024
ATT_TK = 512


def _head_lanes(shape, k):
    lane = lax.broadcasted_iota(jnp.int32, shape, len(shape) - 1)
    return (lane >= k * HEAD_DIM) & (lane < (k + 1) * HEAD_DIM)


def _head_queries(q):
    return [jnp.where(_head_lanes(q.shape, k), q, jnp.zeros_like(q)) for k in range(2)]


def _head_values(v):
    return [jnp.where(_head_lanes(v.shape, k), v, jnp.ones_like(v)) for k in range(2)]


def _online_softmax(s, v_ext, m_ref, acc_ref, h, row0=0):
    reps = s.shape[1] // LANES
    rows = slice(row0, row0 + s.shape[0])
    m_old = m_ref[h, rows]
    m_new = jnp.maximum(m_old, jnp.max(s, axis=-1, keepdims=True))
    alpha = jnp.exp2(m_old - m_new)
    p = jnp.exp2((s - jnp.concatenate([m_new] * reps, axis=1)).astype(BF16))
    acc_ref[h, rows] = alpha * acc_ref[h, rows] + _dot(p, v_ext)
    m_ref[h, rows] = m_new


def _init_softmax(m_ref, acc_ref):
    m_ref[...] = jnp.full(m_ref.shape, NEG, F32)
    acc_ref[...] = jnp.zeros(acc_ref.shape, F32)


def _pair_output(acc_ref, p):
    outs = [acc_ref[2 * p + k] / pltpu.roll(acc_ref[2 * p + k], HEAD_DIM, 1) for k in range(2)]
    return jnp.where(_head_lanes(outs[0].shape, 0), outs[0], outs[1])


def _fox_kernel(q_ref, k_ref, v_ref, cf0_ref, cf1_ref, o_ref, m_ref, acc_ref, *, tk):
    qi = pl.program_id(2)
    tq = q_ref.shape[0]
    _init_softmax(m_ref, acc_ref)
    qh = _head_queries(q_ref[...])

    def tile(j, diagonal):
        rows = pl.ds(pl.multiple_of(j * tk, tk), tk)
        kb = k_ref[rows, :]
        vb = _head_values(v_ref[rows, :])
        row0 = 0 if diagonal is None else diagonal * tk
        for k, cf_ref in enumerate((cf0_ref, cf1_ref)):
            s = _dot_nt(qh[k][row0:], kb) - cf_ref[0, j]
            if diagonal is not None:
                row = lax.broadcasted_iota(jnp.int32, s.shape, 0)
                col = lax.broadcasted_iota(jnp.int32, s.shape, 1)
                s = jnp.where(col <= row, s, NEG)
            _online_softmax(s, vb[k], m_ref, acc_ref, k, row0)

    def body(u, carry):
        tile(2 * u, None)
        tile(2 * u + 1, None)
        return carry

    n_full = qi * (tq // tk)
    lax.fori_loop(0, n_full // 2, body, 0)
    pl.when(n_full % 2 == 1)(lambda: tile(n_full - 1, None))
    for r in range(tq // tk):
        tile(n_full + r, r)
    o_ref[...] = _pair_output(acc_ref, 0)


def _fox_branch(ub, t, bsz, seq):
    n = ub.shape[0]
    tq, tk = min(ATT_TQ, seq), min(ATT_TK, seq)
    nq, nk = seq // tq, seq // tk
    t4 = t.reshape(2 * N_HEADS, bsz * nk, 1, tk)
    kv = lambda col: (lambda b, p, i: (b, col + p))
    cf = lambda k: (lambda b, p, i: (N_HEADS + 2 * p + k, b, 0, 0))
    return pl.pallas_call(
        functools.partial(_fox_kernel, tk=tk),
        grid=(bsz, WIDTH // LANES, nq),
        in_specs=[
            pl.BlockSpec((tq, LANES), lambda b, p, i: (b * nq + i, B_FQ + p)),
            pl.BlockSpec((seq, LANES), kv(B_FK)),
            pl.BlockSpec((seq, LANES), kv(B_FV)),
            pl.BlockSpec((1, nk, 1, tk), cf(0)),
            pl.BlockSpec((1, nk, 1, tk), cf(1)),
        ],
        out_specs=pl.BlockSpec((tq, LANES), lambda b, p, i: (b * nq + i, p)),
        out_shape=jax.ShapeDtypeStruct((n, WIDTH), F32),
        scratch_shapes=[pltpu.VMEM((2, tq, LANES), F32), pltpu.VMEM((2, tq, LANES), F32)],
        compiler_params=_params(("arbitrary",) * 3),
        name="fox_attention",
    )(ub, ub, ub, t4, t4)


BLOCK_MASK = -30000.0


def _moba_kernel(q_ref, k_ref, v_ref, km_ref, o_ref, m_ref, acc_ref, *, tk, n_slots):
    qi = pl.program_id(2)
    tq = q_ref.shape[0]
    blocks_per_tile = tq // MOBA_BLOCK
    _init_softmax(m_ref, acc_ref)
    q = q_ref[...]
    qh = _head_queries(q)
    blk = lax.broadcasted_iota(jnp.int32, (n_slots, tq), 0)
    cur = qi * blocks_per_tile + lax.broadcasted_iota(jnp.int32, (n_slots, tq), 1) // MOBA_BLOCK
    kmb = km_ref[0, 0:n_slots, :].astype(BF16)
    lane_k = lax.broadcasted_iota(jnp.int32, (tk, LANES), 1)
    blk_k = lax.broadcasted_iota(jnp.int32, (tk, LANES), 0) // MOBA_BLOCK

    mask_base = (HEAD_DIM, 0)
    qm = []
    for k in range(2):
        gate = jnp.where(blk < cur, _dot_nt(kmb, qh[k]), -jnp.inf)
        picked = blk == cur
        for _ in range(MOBA_TOPK):
            mx = jnp.max(gate, axis=0, keepdims=True)
            idx = jnp.min(jnp.where(gate == mx, blk, n_slots), axis=0, keepdims=True)
            hit = blk == idx
            picked = picked | (hit & (blk < cur))
            gate = jnp.where(hit, -jnp.inf, gate)
        bias_t = jnp.where(picked, 0.0, BLOCK_MASK)
        bias_t = jnp.concatenate([bias_t, jnp.zeros((LANES - n_slots, tq), F32)], axis=0)
        bias = bias_t.T
        if mask_base[k]:
            bias = pltpu.roll(bias, mask_base[k], 1)
        qm.append(jnp.where(_head_lanes(q.shape, k), q, bias.astype(BF16)))

    def tile(j, diagonal):
        rows = pl.ds(pl.multiple_of(j * tk, tk), tk)
        kb = k_ref[rows, :]
        vb = _head_values(v_ref[rows, :])
        row0 = 0 if diagonal is None else diagonal * tk
        for k in range(2):
            onehot = jnp.where(lane_k == mask_base[k] + j * (tk // MOBA_BLOCK) + blk_k, 1.0, 0.0).astype(BF16)
            s = _dot_nt(qm[k][row0:], jnp.where(_head_lanes(kb.shape, k), kb, onehot))
            if diagonal is not None:
                r2 = lax.broadcasted_iota(jnp.int32, s.shape, 0)
                c2 = lax.broadcasted_iota(jnp.int32, s.shape, 1)
                s = jnp.where((r2 // MOBA_BLOCK == c2 // MOBA_BLOCK) & (c2 > r2), NEG, s)
            _online_softmax(s, vb[k], m_ref, acc_ref, k, row0)

    def body(u, carry):
        tile(2 * u, None)
        tile(2 * u + 1, None)
        return carry

    n_full = qi * (tq // tk)
    lax.fori_loop(0, n_full // 2, body, 0)
    pl.when(n_full % 2 == 1)(lambda: tile(n_full - 1, None))
    for r in range(tq // tk):
        tile(n_full + r, r)
    o_ref[...] = _pair_output(acc_ref, 0)


def _moba_branch(ub, km, bsz, seq):
    n = ub.shape[0]
    tq, tk = min(ATT_TQ, seq), min(ATT_TK, seq)
    nq, nblk = seq // tq, seq // MOBA_BLOCK
    assert nblk <= HEAD_DIM and MOBA_TOPK == 3 and tk % MOBA_BLOCK == 0
    kmp = jnp.zeros((bsz, LANES, WIDTH), F32).at[:, :nblk].set(km.reshape(bsz, nblk, WIDTH))
    kv = lambda col: (lambda b, p, i: (b, col + p))
    return pl.pallas_call(
        functools.partial(_moba_kernel, tk=tk, n_slots=-(-nblk // 16) * 16),
        grid=(bsz, WIDTH // LANES, nq),
        in_specs=[
            pl.BlockSpec((tq, LANES), lambda b, p, i: (b * nq + i, B_MQ + p)),
            pl.BlockSpec((seq, LANES), kv(B_MK)),
            pl.BlockSpec((seq, LANES), kv(B_MV)),
            pl.BlockSpec((1, LANES, LANES), lambda b, p, i: (b, 0, p)),
        ],
        out_specs=pl.BlockSpec((tq, LANES), lambda b, p, i: (b * nq + i, p)),
        out_shape=jax.ShapeDtypeStruct((n, WIDTH), F32),
        scratch_shapes=[pltpu.VMEM((2, tq, LANES), F32), pltpu.VMEM((2, tq, LANES), F32)],
        compiler_params=_params(("arbitrary",) * 3),
        name="moba_attention",
    )(ub, ub, ub, kmp)


DSA_TQ = 512
DSA_TK = 512
COUNT_ROWS = 128
INT_MIN = -2 ** 31


def _dsa_kernel(q_ref, k_ref, v_ref, iq_ref, iw_ref, ik_ref, o_ref,
                m_ref, acc_ref, key_ref, bias_ref, *, tk, seq_bits):
    qi = pl.program_id(1)
    tq = q_ref.shape[0]
    chunks = tk // LANES
    n_need = (qi * tq + tq - 1) // tk + 1
    qpos = qi * tq + lax.broadcasted_iota(jnp.int32, (tq, tk), 0)
    lane_pos = lax.broadcasted_iota(jnp.int32, (tq, tk), 1)
    wide = lambda x: jnp.concatenate([x] * chunks, axis=1)

    _init_softmax(m_ref, acc_ref)
    iq = iq_ref[...]
    iw = iw_ref[...]
    iqh = [_head_queries(iq[:, b * LANES:(b + 1) * LANES]) for b in range(IDX_HEADS // 2)]

    zero_keys = jnp.int32(2 ** seq_bits)

    def score_tile(t, carry):
        kib = ik_ref[pl.ds(pl.multiple_of(t * tk, tk), tk), :]
        isc = jnp.zeros((tq, tk), F32)
        for h in range(IDX_HEADS):
            isc = isc + jnp.maximum(_dot_nt(iqh[h // 2][h % 2], kib), 0.0) * iw[:, h:h + 1]
        kpos = t * tk + lane_pos
        bits = pltpu.bitcast(isc, jnp.int32)
        key = jnp.where(bits < 0, bits ^ jnp.int32(0x7FFFFFFF), bits + zero_keys)
        key = jnp.where(isc == 0.0, zero_keys - kpos, key)
        key_ref[t] = jnp.where(kpos <= qpos, key, INT_MIN)
        return carry

    lax.fori_loop(0, n_need, score_tile, 0)

    @pl.when(n_need % 2 == 1)
    def _():
        key_ref[n_need] = jnp.full((tq, tk), INT_MIN, jnp.int32)

    lane_rb = lax.broadcasted_iota(jnp.int32, (COUNT_ROWS, LANES), 1)

    def count(pred, *row_args):
        accs = []
        for rb in range(tq // COUNT_ROWS):
            rs = slice(rb * COUNT_ROWS, (rb + 1) * COUNT_ROWS)
            args = [a[rs] for a in row_args]

            def one(t, acc):
                for c in range(chunks):
                    kc = key_ref[t, rs, c * LANES:(c + 1) * LANES]
                    hit = pred(kc, t * tk + c * LANES + lane_rb, *args)
                    acc = acc + jnp.where(hit, 1.0, 0.0)
                return acc

            accs.append(lax.fori_loop(0, (n_need + 1) // 2, lambda u, a: one(2 * u + 1, one(2 * u, a)),
                                      jnp.zeros((COUNT_ROWS, LANES), F32)))
        total = jnp.sum(jnp.concatenate(accs, axis=0), axis=-1, keepdims=True)
        return jnp.broadcast_to(total, (tq, LANES))

    def thr_bit(it, carry):
        thr_u, n_at = carry
        cand_u = thr_u | lax.shift_left(jnp.int32(1), 31 - it)
        n_ge = count(lambda key, pos, c: key >= c, cand_u ^ jnp.int32(INT_MIN))
        take = n_ge >= DSA_TOPK
        return jnp.where(take, cand_u, thr_u), jnp.where(take, n_ge, n_at)

    all_keys = jnp.full((tq, LANES), 2.0 * DSA_TOPK, F32)
    thr_u, n_at = lax.fori_loop(0, 32, thr_bit, (jnp.zeros((tq, LANES), jnp.int32), all_keys))
    thr = thr_u ^ jnp.int32(INT_MIN)
    short = thr == INT_MIN
    excess = jnp.where(short, 0.0, n_at - DSA_TOPK)

    def tie_search():
        need = DSA_TOPK - count(lambda key, pos, c: key > c, thr)

        def pos_bit(it, last):
            cand = last | lax.shift_left(jnp.int32(1), seq_bits - 1 - it)
            n_before = count(lambda key, pos, c, th: (key == th) & (pos < c), cand, thr)
            return jnp.where(n_before < need, cand, last)

        return lax.fori_loop(0, seq_bits, pos_bit, jnp.zeros((tq, LANES), jnp.int32))

    everything = jnp.full((tq, LANES), 2 ** seq_bits, jnp.int32)
    last = lax.cond(jnp.max(excess) > 0.0, tie_search, lambda: everything)
    last = jnp.where(short, -1, last)
    thr_w, last_w = wide(thr), wide(last)

    def attend(j, carry):
        rows = pl.ds(pl.multiple_of(j * tk, tk), tk)
        key = key_ref[j]
        keep = (key > thr_w) | ((key == thr_w) & (j * tk + lane_pos <= last_w))
        bias_ref[...] = jnp.where(keep, 0.0, NEG)
        for p in range(WIDTH // LANES):
            sl = slice(p * LANES, (p + 1) * LANES)
            qh = _head_queries(q_ref[:, sl])
            kb = k_ref[rows, sl]
            vb = _head_values(v_ref[rows, sl])
            for k in range(2):
                s = _dot_nt(qh[k], kb) + bias_ref[...]
                _online_softmax(s, vb[k], m_ref, acc_ref, 2 * p + k)
        return carry

    lax.fori_loop(0, n_need, attend, 0)
    for p in range(WIDTH // LANES):
        o_ref[:, p * LANES:(p + 1) * LANES] = _pair_output(acc_ref, p)


def _dsa_branch(ua, ub, bsz, seq):
    n = ub.shape[0]
    tq, tk = min(DSA_TQ, seq), min(DSA_TK, seq)
    nq, nk = seq // tq, seq // tk
    assert seq & (seq - 1) == 0 and DSA_TOPK <= seq // 4
    once = pl.Buffered(1)
    return pl.pallas_call(
        functools.partial(_dsa_kernel, tk=tk, seq_bits=seq.bit_length() - 1),
        grid=(bsz, nq),
        in_specs=[
            pl.BlockSpec((tq, WIDTH), lambda b, i: (b * nq + i, B_DQ * LANES // WIDTH)),
            pl.BlockSpec((seq, WIDTH), lambda b, i: (b, B_DK * LANES // WIDTH), pipeline_mode=once),
            pl.BlockSpec((seq, WIDTH), lambda b, i: (b, B_DV * LANES // WIDTH), pipeline_mode=once),
            pl.BlockSpec((tq, 2 * LANES), lambda b, i: (b * nq + i, B_IQ // 2)),
            pl.BlockSpec((tq, LANES), lambda b, i: (b * nq + i, A_IW)),
            pl.BlockSpec((seq, LANES), lambda b, i: (b, B_IK), pipeline_mode=once),
        ],
        out_specs=pl.BlockSpec((tq, WIDTH), lambda b, i: (b * nq + i, 0)),
        out_shape=jax.ShapeDtypeStruct((n, WIDTH), F32),
        scratch_shapes=[pltpu.VMEM((N_HEADS, tq, LANES), F32), pltpu.VMEM((N_HEADS, tq, LANES), F32),
                        pltpu.VMEM((nk + nk % 2, tq, tk), jnp.int32),
                        pltpu.VMEM((tq, tk), F32)],
        compiler_params=_params(("arbitrary",) * 2),
        name="dsa_attention",
    )(ub, ub, ub, ub, ua, ub)


def _merge_kernel(x_ref, gpre_ref, y0_ref, y1_ref, y2_ref, y3_ref, wg_ref, bg_ref, wb_ref, wo_ref,
                  gpost_ref, o_ref):
    x = x_ref[...]
    d = x.shape[1]
    hb = _rms(x, gpre_ref[...]).astype(BF16)
    merged = jnp.zeros(x.shape, F32)
    for n, y_ref in enumerate((y0_ref, y1_ref, y2_ref, y3_ref)):
        logit = _dot(hb, wg_ref[:, n * d:(n + 1) * d]) + bg_ref[:, n * d:(n + 1) * d]
        gate = 1.0 / (1.0 + jnp.exp(-logit))
        merged = merged + gate * _dot(y_ref[...].astype(BF16), wb_ref[n])
    mix = _dot(merged.astype(BF16), wo_ref[...])
    o_ref[...] = x + _rms(mix, gpost_ref[...])


def _merge(x2, g_pre, ys, w_gate, b_gate, w_branch, w_out, g_post):
    n, d = x2.shape
    tm = MLP_TILE
    const2 = lambda i: (0, 0)
    row = lambda i: (i, 0)
    once = pl.Buffered(1)
    return pl.pallas_call(
        _merge_kernel,
        grid=(n // tm,),
        in_specs=[pl.BlockSpec((tm, d), row), pl.BlockSpec((1, d), const2)]
        + [pl.BlockSpec((tm, WIDTH), row)] * N_BRANCH
        + [pl.BlockSpec((d, N_BRANCH * d), const2, pipeline_mode=once),
           pl.BlockSpec((1, N_BRANCH * d), const2),
           pl.BlockSpec((N_BRANCH, WIDTH, d), lambda i: (0, 0, 0), pipeline_mode=once),
           pl.BlockSpec((d, d), const2, pipeline_mode=once),
           pl.BlockSpec((1, d), const2)],
        out_specs=pl.BlockSpec((tm, d), row),
        out_shape=jax.ShapeDtypeStruct((n, d), F32),
        compiler_params=_params(("arbitrary",)),
        name="branch_merge",
    )(x2, g_pre.reshape(1, d), *ys, w_gate.astype(BF16), b_gate.reshape(1, -1),
      w_branch.astype(BF16), w_out.astype(BF16), g_post.reshape(1, d))


FFN_CHUNK = 256


def _ffn_kernel(x_ref, gpre_ref, wi_ref, wo_ref, gpost_ref, o_ref):
    x = x_ref[...]
    hb = _rms(x, gpre_ref[...]).astype(BF16)
    acc = jnp.zeros(x.shape, F32)
    for c in range(FFN_HIDDEN // FFN_CHUNK):
        lo = c * FFN_CHUNK
        gate = _dot(hb, wi_ref[:, lo:lo + FFN_CHUNK])
        up = _dot(hb, wi_ref[:, FFN_HIDDEN + lo:FFN_HIDDEN + lo + FFN_CHUNK])
        acc = acc + _dot((_silu(gate) * up).astype(BF16), wo_ref[lo:lo + FFN_CHUNK, :])
    o_ref[...] = x + _rms(acc, gpost_ref[...])


def _ffn(x2, g_pre, w_ffn_in, w_ffn_out, g_post):
    n, d = x2.shape
    tm = MLP_TILE
    const2 = lambda i: (0, 0)
    row = lambda i: (i, 0)
    once = pl.Buffered(1)
    return pl.pallas_call(
        _ffn_kernel,
        grid=(n // tm,),
        in_specs=[pl.BlockSpec((tm, d), row), pl.BlockSpec((1, d), const2),
                  pl.BlockSpec((d, 2 * FFN_HIDDEN), const2, pipeline_mode=once),
                  pl.BlockSpec((FFN_HIDDEN, d), const2, pipeline_mode=once),
                  pl.BlockSpec((1, d), const2)],
        out_specs=pl.BlockSpec((tm, d), row),
        out_shape=jax.ShapeDtypeStruct((n, d), F32),
        compiler_params=_params(("arbitrary",)),
        name="swiglu_ffn",
    )(x2, g_pre.reshape(1, d), w_ffn_in.astype(BF16), w_ffn_out.astype(BF16), g_post.reshape(1, d))


def kernel(x, norm_mix_pre, w_in, conv_w, conv_b, dt_bias, a_log, d_skip, ssm_norm, fox_fbias, w_gate, b_gate, w_branch, w_out, norm_mix_post, norm_ffn_pre, w_ffn_in, w_ffn_out, norm_ffn_post):
    bsz, seq, d = x.shape
    x2 = x.reshape(bsz * seq, d)
    cos, sin = _rope_tables(seq)
    for l in range(w_in.shape[0]):
        w, wt = _rearranged_w_in(w_in[l])
        ua, ub, t, km = _in_projection(x2, norm_mix_pre[l], w, wt, cos, sin, fox_fbias[l], seq)
        ys = (_ssd_branch(ua, t, conv_w[l], conv_b[l], dt_bias[l], a_log[l], d_skip[l], ssm_norm[l], bsz, seq),
              _fox_branch(ub, t, bsz, seq),
              _moba_branch(ub, km, bsz, seq),
              _dsa_branch(ua, ub, bsz, seq))
        x2 = _merge(x2, norm_mix_pre[l], ys, w_gate[l], b_gate[l], w_branch[l], w_out[l], norm_mix_post[l])
        x2 = _ffn(x2, norm_ffn_pre[l], w_ffn_in[l], w_ffn_out[l], norm_ffn_post[l])
    return x2.reshape(bsz, seq, d)
```

```python
import functools
import math

import jax
import jax.numpy as jnp
from jax import lax
from jax.experimental import pallas as pl
from jax.experimental.pallas import tpu as pltpu

F32 = jnp.float32
BF16 = jnp.bfloat16

D_MODEL = 1024
HEAD_DIM = 64
N_HEADS = 8
WIDTH = N_HEADS * HEAD_DIM
N_BRANCH = 4
SSM_GROUPS = 2
SSM_STATE = 128
SSM_CONV = 4
SSM_CHUNK = 256
MOBA_BLOCK = 256
MOBA_TOPK = 3
IDX_HEADS = 4
IDX_DIM = 64
DSA_TOPK = 256
ROPE_THETA = 10000.0
FFN_HIDDEN = -(-8 * D_MODEL // (3 * 256)) * 256
EPS = 1e-6

LANES = 128
NEG = -1e30
VMEM_LIMIT = 56 * 1024 * 1024

LOG2E = math.log2(math.e)

A_XBC, A_Z, A_DT, A_IW, NA = 0, 8, 12, 13, 14
B_DV, B_FQ, B_FK, B_FV, B_MV = 0, 4, 8, 12, 16
B_DQ, B_DK, B_MQ, B_MK, B_IQ, B_IK, NB = 20, 24, 28, 32, 36, 38, 40
B_ROPE = B_DQ
N_COLBLK = NA + NB
PROJ_TN = 256
ROW_TILE = 512
MLP_TILE = 512


def _dot(a, b):
    return jnp.dot(a, b, preferred_element_type=F32)


def _dot_nt(a, b):
    return lax.dot_general(a, b, (((1,), (1,)), ((), ())), preferred_element_type=F32)


def _dot_tn(a, b):
    return lax.dot_general(a, b, (((0,), (0,)), ((), ())), preferred_element_type=F32)


def _dot_f32(a, b):
    return jnp.dot(a, b, preferred_element_type=F32, precision=lax.Precision.HIGHEST)


def _rms(x, g):
    return x * lax.rsqrt(jnp.mean(x * x, axis=-1, keepdims=True) + EPS) * g


def _silu(x):
    return x * (1.0 / (1.0 + jnp.exp(-x)))


def _softplus(x):
    return jnp.maximum(x, 0.0) + jnp.log(1.0 + jnp.exp(-jnp.abs(x)))


def _params(sem):
    return pltpu.CompilerParams(dimension_semantics=sem, vmem_limit_bytes=VMEM_LIMIT)


def _inproj_kernel(x_ref, g_ref, w_ref, wt_ref, cos_ref, sin_ref, fb_ref, tri_ref,
                   ua_ref, ub_ref, t_ref, km_ref, carry_ref, *, tiles_per_seq):
    i = pl.program_id(0)
    hb = _rms(x_ref[...], g_ref[...]).astype(BF16)
    tm = hb.shape[0]
    lane = lax.broadcasted_iota(jnp.int32, (tm, PROJ_TN), 1)
    first_half = (lane % HEAD_DIM) < (HEAD_DIM // 2)
    n_tiles = N_COLBLK * LANES // PROJ_TN
    for j in range(n_tiles):
        c0 = j * PROJ_TN
        acc = _dot(hb, w_ref[:, c0:c0 + PROJ_TN])
        blk = c0 // LANES
        if blk < NA:
            ua_ref[:, c0:c0 + PROJ_TN] = acc
            continue
        blk -= NA
        if blk >= B_ROPE:
            partner = jnp.where(first_half,
                                pltpu.roll(acc, PROJ_TN - HEAD_DIM // 2, 1),
                                pltpu.roll(acc, HEAD_DIM // 2, 1))
            cos = jnp.concatenate([cos_ref[...]] * (PROJ_TN // LANES), axis=1)
            sin = jnp.concatenate([sin_ref[...]] * (PROJ_TN // LANES), axis=1)
            acc = acc * cos + partner * sin
            if B_MK <= blk < B_MK + WIDTH // LANES:
                o = (blk - B_MK) * LANES
                for r in range(tm // MOBA_BLOCK):
                    km_ref[r, :, o:o + PROJ_TN] = jnp.mean(acc[r * MOBA_BLOCK:(r + 1) * MOBA_BLOCK],
                                                           axis=0, keepdims=True)
        ub_ref[:, blk * LANES:blk * LANES + PROJ_TN] = acc.astype(BF16)

    t = _dot_nt(wt_ref[...], hb)

    @pl.when(i % tiles_per_seq == 0)
    def _():
        carry_ref[...] = jnp.zeros_like(carry_ref)

    logf = -_softplus(-(t[N_HEADS:] + fb_ref[...]))
    cf = _dot_f32(logf, tri_ref[...]) + carry_ref[...]
    carry_ref[...] = cf[:, tm - 1:tm]
    t_ref[0:N_HEADS, :] = t[:N_HEADS]
    t_ref[N_HEADS:, :] = cf * LOG2E


def _rearranged_w_in(w_in):
    o = 0
    parts = {}
    for name, wdt in (("z", 512), ("xs", 512), ("bs", 256), ("cs", 256), ("dt", 8),
                      ("fq", 512), ("fk", 512), ("fv", 512), ("ff", 8),
                      ("mq", 512), ("mk", 512), ("mv", 512),
                      ("dq", 512), ("dk", 512), ("dv", 512),
                      ("iq", 256), ("ik", 64), ("iw", 4)):
        parts[name] = w_in[:, o:o + wdt]
        o += wdt
    d = w_in.shape[0]
    pad = lambda n: jnp.zeros((d, n), w_in.dtype)
    qs = HEAD_DIM ** -0.5 * LOG2E
    cols = [parts["xs"], parts["bs"], parts["cs"], parts["z"],
            parts["dt"], pad(LANES - 8),
            parts["iw"] * (IDX_HEADS ** -0.5), pad(LANES - 4),
            parts["dv"], parts["fq"] * qs, parts["fk"], parts["fv"], parts["mv"],
            parts["dq"] * qs, parts["dk"], parts["mq"] * qs, parts["mk"],
            parts["iq"] * (IDX_DIM ** -0.5), parts["ik"], parts["ik"], pad(LANES)]
    w = jnp.concatenate(cols, axis=1).astype(BF16)
    assert w.shape[1] == N_COLBLK * LANES
    wt = jnp.concatenate([parts["dt"], parts["ff"]], axis=1).T.astype(BF16)
    return w, wt


def _rope_tables(seq):
    pos = jnp.arange(seq, dtype=F32)
    inv = ROPE_THETA ** (-jnp.arange(0, HEAD_DIM, 2, dtype=F32) / HEAD_DIM)
    ang = pos[:, None] * inv[None, :]
    c, s = jnp.cos(ang), jnp.sin(ang)
    cos = jnp.concatenate([c, c, c, c], axis=1)
    sin = jnp.concatenate([-s, s, -s, s], axis=1)
    return cos, sin


def _in_projection(x2, g, w, wt, cos, sin, fbias, seq):
    n, d = x2.shape
    tm = ROW_TILE
    tiles_per_seq = seq // tm
    ncol = N_COLBLK * LANES
    tri = jnp.triu(jnp.ones((tm, tm), F32))
    const = lambda i: (0, 0)
    return pl.pallas_call(
        functools.partial(_inproj_kernel, tiles_per_seq=tiles_per_seq),
        grid=(n // tm,),
        in_specs=[
            pl.BlockSpec((tm, d), lambda i: (i, 0)),
            pl.BlockSpec((1, d), const),
            pl.BlockSpec((d, ncol), const, pipeline_mode=pl.Buffered(1)),
            pl.BlockSpec((2 * N_HEADS, d), const),
            pl.BlockSpec((tm, LANES), lambda i: (i % tiles_per_seq, 0)),
            pl.BlockSpec((tm, LANES), lambda i: (i % tiles_per_seq, 0)),
            pl.BlockSpec((N_HEADS, 1), const),
            pl.BlockSpec((tm, tm), const),
        ],
        out_specs=[
            pl.BlockSpec((tm, NA * LANES), lambda i: (i, 0)),
            pl.BlockSpec((tm, NB * LANES), lambda i: (i, 0)),
            pl.BlockSpec((2 * N_HEADS, tm), lambda i: (0, i)),
            pl.BlockSpec((tm // MOBA_BLOCK, 1, WIDTH), lambda i: (i, 0, 0)),
        ],
        out_shape=[
            jax.ShapeDtypeStruct((n, NA * LANES), F32),
            jax.ShapeDtypeStruct((n, NB * LANES), BF16),
            jax.ShapeDtypeStruct((2 * N_HEADS, n), F32),
            jax.ShapeDtypeStruct((n // MOBA_BLOCK, 1, WIDTH), F32),
        ],
        scratch_shapes=[pltpu.VMEM((N_HEADS, 1), F32)],
        compiler_params=_params(("arbitrary",)),
        name="in_projection",
    )(x2, g.reshape(1, d), w, wt, cos, sin, fbias.reshape(N_HEADS, 1), tri)


def _ssd_kernel(z_ref, xbc_ref, dtc_ref, t_ref, cw_ref, cb_ref, dtb_row_ref, dtb_col_ref,
                alog_row_ref, alog_col_ref, dskip_ref, nw_ref, tril_ref, triu_ref, expand_ref,
                y_ref, ext_ref, state_ref):
    c = pl.program_id(1)
    q = SSM_CHUNK
    halo = 8
    n_pair = WIDTH // LANES

    @pl.when(c == 0)
    def _():
        ext_ref[0:halo, :] = jnp.zeros((halo, ext_ref.shape[1]), F32)
        state_ref[...] = jnp.zeros_like(state_ref)

    @pl.when(c > 0)
    def _():
        ext_ref[0:halo, :] = ext_ref[q:q + halo, :]

    ext_ref[halo:halo + q, :] = xbc_ref[...]
    conv = cb_ref[...]
    for w in range(SSM_CONV):
        o = halo - (SSM_CONV - 1) + w
        conv = conv + ext_ref[o:o + q, :] * cw_ref[w:w + 1, :]
    xbc = _silu(conv)
    xs = xbc[:, :WIDTH]
    bmat = xbc[:, WIDTH:WIDTH + SSM_GROUPS * SSM_STATE].astype(BF16)
    cmat = xbc[:, WIDTH + SSM_GROUPS * SSM_STATE:].astype(BF16)

    dt_col = _softplus(dtc_ref[...] + dtb_row_ref[...])
    dt_row = _softplus(t_ref[0:N_HEADS, :] + dtb_col_ref[...])
    a_row = -jnp.exp(alog_row_ref[...])
    a_col = -jnp.exp(alog_col_ref[...])
    acs_col = _dot_f32(tril_ref[...], dt_col * a_row)
    acs_row = _dot_f32(dt_row * a_col, triu_ref[...])
    alast_row = acs_col[q - 1:q, :]
    alast_col = acs_row[:, q - 1:q]

    expand = expand_ref[...]
    xdt = xs * _dot_f32(dt_col, expand)
    dstate_x = _dot_f32(jnp.exp(alast_row - acs_col), expand)
    ea_x = _dot_f32(jnp.exp(acs_col), expand)

    li = lax.broadcasted_iota(jnp.int32, (q, q), 0)
    si = lax.broadcasted_iota(jnp.int32, (q, q), 1)
    causal = li >= si
    lane = lax.broadcasted_iota(jnp.int32, (q, LANES), 1)
    sub = lax.broadcasted_iota(jnp.int32, (LANES, 1), 0)

    cb = []
    for g in range(SSM_GROUPS):
        cg = cmat[:, g * SSM_STATE:(g + 1) * SSM_STATE]
        bg = bmat[:, g * SSM_STATE:(g + 1) * SSM_STATE]
        cb.append(_dot_nt(cg, bg))

    for p in range(n_pair):
        g = (2 * p) // (N_HEADS // SSM_GROUPS)
        cg = cmat[:, g * SSM_STATE:(g + 1) * SSM_STATE]
        bg = bmat[:, g * SSM_STATE:(g + 1) * SSM_STATE]
        sl = slice(p * LANES, (p + 1) * LANES)
        xdt_p = xdt[:, sl]
        y = jnp.zeros((q, LANES), F32)
        for k in range(2):
            h = 2 * p + k
            seg = acs_col[:, h:h + 1] - acs_row[h:h + 1, :]
            decay = jnp.exp(jnp.where(causal, seg, NEG))
            scores = (cb[g] * decay).astype(BF16)
            in_head = (lane >= k * HEAD_DIM) & (lane < (k + 1) * HEAD_DIM)
            y = y + _dot(scores, jnp.where(in_head, xdt_p, 0.0).astype(BF16))
        prev = state_ref[p]
        y = y + _dot_nt(cg, prev.astype(BF16)) * ea_x[:, sl]
        contrib = _dot_tn((xdt_p * dstate_x[:, sl]).astype(BF16), bg)
        cd = jnp.exp(jnp.where(sub < HEAD_DIM, alast_col[2 * p:2 * p + 1, :],
                               alast_col[2 * p + 1:2 * p + 2, :]))
        state_ref[p] = prev * cd + contrib
        y = y + dskip_ref[:, sl] * xs[:, sl]
        y_ref[:, sl] = y * _silu(z_ref[:, sl])

    gw = WIDTH // SSM_GROUPS
    for g in range(SSM_GROUPS):
        yg = y_ref[:, g * gw:(g + 1) * gw]
        y_ref[:, g * gw:(g + 1) * gw] = _rms(yg, nw_ref[:, g * gw:(g + 1) * gw])


def _ssd_branch(u, t, conv_w, conv_b, dt_bias, a_log, d_skip, ssm_norm, bsz, seq):
    n = u.shape[0]
    q = SSM_CHUNK
    nc = seq // q
    cch = conv_w.shape[1]
    pad_row = lambda v: jnp.zeros((1, LANES), F32).at[0, :N_HEADS].set(v)
    expand = (jnp.arange(LANES)[:, None] == (jnp.arange(WIDTH)[None, :] // HEAD_DIM)).astype(F32)
    tril = jnp.tril(jnp.ones((q, q), F32))
    const = lambda b, c: (0, 0)
    row = lambda b, c: (b * nc + c, 0)
    return pl.pallas_call(
        _ssd_kernel,
        grid=(bsz, nc),
        in_specs=[
            pl.BlockSpec((q, WIDTH), lambda b, c: (b * nc + c, A_Z * LANES // WIDTH)),
            pl.BlockSpec((q, cch), lambda b, c: (b * nc + c, A_XBC * LANES // cch)),
            pl.BlockSpec((q, LANES), lambda b, c: (b * nc + c, A_DT)),
            pl.BlockSpec((2 * N_HEADS, q), lambda b, c: (0, b * nc + c)),
            pl.BlockSpec((SSM_CONV, cch), const),
            pl.BlockSpec((1, cch), const),
            pl.BlockSpec((1, LANES), const),
            pl.BlockSpec((N_HEADS, 1), const),
            pl.BlockSpec((1, LANES), const),
            pl.BlockSpec((N_HEADS, 1), const),
            pl.BlockSpec((1, WIDTH), const),
            pl.BlockSpec((1, WIDTH), const),
            pl.BlockSpec((q, q), const),
            pl.BlockSpec((q, q), const),
            pl.BlockSpec((LANES, WIDTH), const),
        ],
        out_specs=pl.BlockSpec((q, WIDTH), row),
        out_shape=jax.ShapeDtypeStruct((n, WIDTH), F32),
        scratch_shapes=[pltpu.VMEM((q + 8, cch), F32),
                        pltpu.VMEM((WIDTH // LANES, LANES, SSM_STATE), F32)],
        compiler_params=_params(("arbitrary", "arbitrary")),
        name="ssd_branch",
    )(u, u, u, t, conv_w, conv_b.reshape(1, cch), pad_row(dt_bias), dt_bias.reshape(N_HEADS, 1),
      pad_row(a_log), a_log.reshape(N_HEADS, 1), jnp.repeat(d_skip, HEAD_DIM).reshape(1, WIDTH),
      ssm_norm.reshape(1, WIDTH), tril, tril.T, expand)


ATT_TQ = 1024
ATT_TK = 512


def _head_lanes(shape, k):
    lane = lax.broadcasted_iota(jnp.int32, shape, len(shape) - 1)
    return (lane >= k * HEAD_DIM) & (lane < (k + 1) * HEAD_DIM)


def _head_queries(q):
    return [jnp.where(_head_lanes(q.shape, k), q, jnp.zeros_like(q)) for k in range(2)]


def _head_values(v):
    return [jnp.where(_head_lanes(v.shape, k), v, jnp.ones_like(v)) for k in range(2)]


def _online_softmax(s, v_ext, m_ref, acc_ref, h, row0=0):
    reps = s.shape[1] // LANES
    rows = slice(row0, row0 + s.shape[0])
    m_old = m_ref[h, rows]
    m_new = jnp.maximum(m_old, jnp.max(s, axis=-1, keepdims=True))
    alpha = jnp.exp2(m_old - m_new)
    p = jnp.exp2((s - jnp.concatenate([m_new] * reps, axis=1)).astype(BF16))
    acc_ref[h, rows] = alpha * acc_ref[h, rows] + _dot(p, v_ext)
    m_ref[h, rows] = m_new


def _init_softmax(m_ref, acc_ref):
    m_ref[...] = jnp.full(m_ref.shape, NEG, F32)
    acc_ref[...] = jnp.zeros(acc_ref.shape, F32)


def _pair_output(acc_ref, p):
    outs = [acc_ref[2 * p + k] / pltpu.roll(acc_ref[2 * p + k], HEAD_DIM, 1) for k in range(2)]
    return jnp.where(_head_lanes(outs[0].shape, 0), outs[0], outs[1])


def _fox_kernel(q_ref, k_ref, v_ref, cf0_ref, cf1_ref, o_ref, m_ref, acc_ref, *, tk):
    qi = pl.program_id(2)
    tq = q_ref.shape[0]
    _init_softmax(m_ref, acc_ref)
    qh = _head_queries(q_ref[...])

    def tile(j, diagonal):
        rows = pl.ds(pl.multiple_of(j * tk, tk), tk)
        kb = k_ref[rows, :]
        vb = _head_values(v_ref[rows, :])
        row0 = 0 if diagonal is None else diagonal * tk
        for k, cf_ref in enumerate((cf0_ref, cf1_ref)):
            s = _dot_nt(qh[k][row0:], kb) - cf_ref[0, j]
            if diagonal is not None:
                row = lax.broadcasted_iota(jnp.int32, s.shape, 0)
                col = lax.broadcasted_iota(jnp.int32, s.shape, 1)
                s = jnp.where(col <= row, s, NEG)
            _online_softmax(s, vb[k], m_ref, acc_ref, k, row0)

    def body(u, carry):
        tile(2 * u, None)
        tile(2 * u + 1, None)
        return carry

    n_full = qi * (tq // tk)
    lax.fori_loop(0, n_full // 2, body, 0)
    pl.when(n_full % 2 == 1)(lambda: tile(n_full - 1, None))
    for r in range(tq // tk):
        tile(n_full + r, r)
    o_ref[...] = _pair_output(acc_ref, 0).astype(o_ref.dtype)


def _fox_branch(ub, t, bsz, seq):
    n = ub.shape[0]
    tq, tk = min(ATT_TQ, seq), min(ATT_TK, seq)
    nq, nk = seq // tq, seq // tk
    t4 = t.reshape(2 * N_HEADS, bsz * nk, 1, tk)
    kv = lambda col: (lambda b, p, i: (b, col + p))
    cf = lambda k: (lambda b, p, i: (N_HEADS + 2 * p + k, b, 0, 0))
    return pl.pallas_call(
        functools.partial(_fox_kernel, tk=tk),
        grid=(bsz, WIDTH // LANES, nq),
        in_specs=[
            pl.BlockSpec((tq, LANES), lambda b, p, i: (b * nq + i, B_FQ + p)),
            pl.BlockSpec((seq, LANES), kv(B_FK)),
            pl.BlockSpec((seq, LANES), kv(B_FV)),
            pl.BlockSpec((1, nk, 1, tk), cf(0)),
            pl.BlockSpec((1, nk, 1, tk), cf(1)),
        ],
        out_specs=pl.BlockSpec((tq, LANES), lambda b, p, i: (b * nq + i, p)),
        out_shape=jax.ShapeDtypeStruct((n, WIDTH), BF16),
        scratch_shapes=[pltpu.VMEM((2, tq, LANES), F32), pltpu.VMEM((2, tq, LANES), F32)],
        compiler_params=_params(("arbitrary",) * 3),
        name="fox_attention",
    )(ub, ub, ub, t4, t4)


BLOCK_MASK = -30000.0


def _moba_kernel(q_ref, k_ref, v_ref, km_ref, o_ref, m_ref, acc_ref, *, tk, n_slots):
    qi = pl.program_id(2)
    tq = q_ref.shape[0]
    blocks_per_tile = tq // MOBA_BLOCK
    _init_softmax(m_ref, acc_ref)
    q = q_ref[...]
    qh = _head_queries(q)
    blk = lax.broadcasted_iota(jnp.int32, (n_slots, tq), 0)
    cur = qi * blocks_per_tile + lax.broadcasted_iota(jnp.int32, (n_slots, tq), 1) // MOBA_BLOCK
    kmb = km_ref[0, 0:n_slots, :].astype(BF16)
    lane_k = lax.broadcasted_iota(jnp.int32, (tk, LANES), 1)
    blk_k = lax.broadcasted_iota(jnp.int32, (tk, LANES), 0) // MOBA_BLOCK

    mask_base = (HEAD_DIM, 0)
    qm = []
    for k in range(2):
        gate = jnp.where(blk < cur, _dot_nt(kmb, qh[k]), -jnp.inf)
        picked = blk == cur
        for _ in range(MOBA_TOPK):
            mx = jnp.max(gate, axis=0, keepdims=True)
            idx = jnp.min(jnp.where(gate == mx, blk, n_slots), axis=0, keepdims=True)
            hit = blk == idx
            picked = picked | (hit & (blk < cur))
            gate = jnp.where(hit, -jnp.inf, gate)
        bias_t = jnp.where(picked, 0.0, BLOCK_MASK)
        bias_t = jnp.concatenate([bias_t, jnp.zeros((LANES - n_slots, tq), F32)], axis=0)
        bias = bias_t.T
        if mask_base[k]:
            bias = pltpu.roll(bias, mask_base[k], 1)
        qm.append(jnp.where(_head_lanes(q.shape, k), q, bias.astype(BF16)))

    def tile(j, diagonal):
        rows = pl.ds(pl.multiple_of(j * tk, tk), tk)
        kb = k_ref[rows, :]
        vb = _head_values(v_ref[rows, :])
        row0 = 0 if diagonal is None else diagonal * tk
        for k in range(2):
            onehot = jnp.where(lane_k == mask_base[k] + j * (tk // MOBA_BLOCK) + blk_k, 1.0, 0.0).astype(BF16)
            s = _dot_nt(qm[k][row0:], jnp.where(_head_lanes(kb.shape, k), kb, onehot))
            if diagonal is not None:
                r2 = lax.broadcasted_iota(jnp.int32, s.shape, 0)
                c2 = lax.broadcasted_iota(jnp.int32, s.shape, 1)
                s = jnp.where((r2 // MOBA_BLOCK == c2 // MOBA_BLOCK) & (c2 > r2), NEG, s)
            _online_softmax(s, vb[k], m_ref, acc_ref, k, row0)

    def body(u, carry):
        tile(2 * u, None)
        tile(2 * u + 1, None)
        return carry

    n_full = qi * (tq // tk)
    lax.fori_loop(0, n_full // 2, body, 0)
    pl.when(n_full % 2 == 1)(lambda: tile(n_full - 1, None))
    for r in range(tq // tk):
        tile(n_full + r, r)
    o_ref[...] = _pair_output(acc_ref, 0).astype(o_ref.dtype)


def _moba_branch(ub, km, bsz, seq):
    n = ub.shape[0]
    tq, tk = min(ATT_TQ, seq), min(ATT_TK, seq)
    nq, nblk = seq // tq, seq // MOBA_BLOCK
    assert nblk <= HEAD_DIM and MOBA_TOPK == 3 and tk % MOBA_BLOCK == 0
    kmp = jnp.zeros((bsz, LANES, WIDTH), F32).at[:, :nblk].set(km.reshape(bsz, nblk, WIDTH))
    kv = lambda col: (lambda b, p, i: (b, col + p))
    return pl.pallas_call(
        functools.partial(_moba_kernel, tk=tk, n_slots=-(-nblk // 16) * 16),
        grid=(bsz, WIDTH // LANES, nq),
        in_specs=[
            pl.BlockSpec((tq, LANES), lambda b, p, i: (b * nq + i, B_MQ + p)),
            pl.BlockSpec((seq, LANES), kv(B_MK)),
            pl.BlockSpec((seq, LANES), kv(B_MV)),
            pl.BlockSpec((1, LANES, LANES), lambda b, p, i: (b, 0, p)),
        ],
        out_specs=pl.BlockSpec((tq, LANES), lambda b, p, i: (b * nq + i, p)),
        out_shape=jax.ShapeDtypeStruct((n, WIDTH), BF16),
        scratch_shapes=[pltpu.VMEM((2, tq, LANES), F32), pltpu.VMEM((2, tq, LANES), F32)],
        compiler_params=_params(("arbitrary",) * 3),
        name="moba_attention",
    )(ub, ub, ub, kmp)


DSA_TQ = 512
DSA_TK = 512
COUNT_ROWS = 128
INT_MIN = -2 ** 31


def _dsa_kernel(q_ref, k_ref, v_ref, iq_ref, iw_ref, ik_ref, o_ref,
                m_ref, acc_ref, key_ref, bias_ref, *, tk, seq_bits):
    qi = pl.program_id(1)
    tq = q_ref.shape[0]
    chunks = tk // LANES
    n_need = (qi * tq + tq - 1) // tk + 1
    qpos = qi * tq + lax.broadcasted_iota(jnp.int32, (tq, tk), 0)
    lane_pos = lax.broadcasted_iota(jnp.int32, (tq, tk), 1)
    wide = lambda x: jnp.concatenate([x] * chunks, axis=1)

    _init_softmax(m_ref, acc_ref)
    iq = iq_ref[...]
    iw = iw_ref[...]
    iqh = [_head_queries(iq[:, b * LANES:(b + 1) * LANES]) for b in range(IDX_HEADS // 2)]

    zero_keys = jnp.int32(2 ** seq_bits)

    def score_tile(t, carry):
        kib = ik_ref[pl.ds(pl.multiple_of(t * tk, tk), tk), :]
        isc = jnp.zeros((tq, tk), F32)
        for h in range(IDX_HEADS):
            isc = isc + jnp.maximum(_dot_nt(iqh[h // 2][h % 2], kib), 0.0) * iw[:, h:h + 1]
        kpos = t * tk + lane_pos
        bits = pltpu.bitcast(isc, jnp.int32)
        key = jnp.where(bits < 0, bits ^ jnp.int32(0x7FFFFFFF), bits + zero_keys)
        key = jnp.where(isc == 0.0, zero_keys - kpos, key)
        key_ref[t] = jnp.where(kpos <= qpos, key, INT_MIN)
        return carry

    lax.fori_loop(0, n_need, score_tile, 0)

    @pl.when(n_need % 2 == 1)
    def _():
        key_ref[n_need] = jnp.full((tq, tk), INT_MIN, jnp.int32)

    lane_rb = lax.broadcasted_iota(jnp.int32, (COUNT_ROWS, LANES), 1)

    def count(pred, *row_args):
        accs = []
        for rb in range(tq // COUNT_ROWS):
            rs = slice(rb * COUNT_ROWS, (rb + 1) * COUNT_ROWS)
            args = [a[rs] for a in row_args]

            def one(t, acc):
                for c in range(chunks):
                    kc = key_ref[t, rs, c * LANES:(c + 1) * LANES]
                    hit = pred(kc, t * tk + c * LANES + lane_rb, *args)
                    acc = acc + jnp.where(hit, 1.0, 0.0)
                return acc

            accs.append(lax.fori_loop(0, (n_need + 1) // 2, lambda u, a: one(2 * u + 1, one(2 * u, a)),
                                      jnp.zeros((COUNT_ROWS, LANES), F32)))
        total = jnp.sum(jnp.concatenate(accs, axis=0), axis=-1, keepdims=True)
        return jnp.broadcast_to(total, (tq, LANES))

    def thr_bit(it, carry):
        thr_u, n_at = carry
        cand_u = thr_u | lax.shift_left(jnp.int32(1), 31 - it)
        n_ge = count(lambda key, pos, c: key >= c, cand_u ^ jnp.int32(INT_MIN))
        take = n_ge >= DSA_TOPK
        return jnp.where(take, cand_u, thr_u), jnp.where(take, n_ge, n_at)

    all_keys = jnp.full((tq, LANES), 2.0 * DSA_TOPK, F32)
    thr_u, n_at = lax.fori_loop(0, 32, thr_bit, (jnp.zeros((tq, LANES), jnp.int32), all_keys))
    thr = thr_u ^ jnp.int32(INT_MIN)
    short = thr == INT_MIN
    excess = jnp.where(short, 0.0, n_at - DSA_TOPK)

    def tie_search():
        need = DSA_TOPK - count(lambda key, pos, c: key > c, thr)

        def pos_bit(it, last):
            cand = last | lax.shift_left(jnp.int32(1), seq_bits - 1 - it)
            n_before = count(lambda key, pos, c, th: (key == th) & (pos < c), cand, thr)
            return jnp.where(n_before < need, cand, last)

        return lax.fori_loop(0, seq_bits, pos_bit, jnp.zeros((tq, LANES), jnp.int32))

    everything = jnp.full((tq, LANES), 2 ** seq_bits, jnp.int32)
    last = lax.cond(jnp.max(excess) > 0.0, tie_search, lambda: everything)
    last = jnp.where(short, -1, last)
    thr_w, last_w = wide(thr), wide(last)

    def attend(j):
        rows = pl.ds(pl.multiple_of(j * tk, tk), tk)
        key = key_ref[j]
        keep = (key > thr_w) | ((key == thr_w) & (j * tk + lane_pos <= last_w))
        bias_ref[...] = jnp.where(keep, 0.0, NEG)
        for p in range(WIDTH // LANES):
            sl = slice(p * LANES, (p + 1) * LANES)
            s_both = _dot_nt(jnp.concatenate(_head_queries(q_ref[:, sl]), axis=0), k_ref[rows, sl])
            vb = _head_values(v_ref[rows, sl])
            for k in range(2):
                s = s_both[k * tq:(k + 1) * tq] + bias_ref[...]
                _online_softmax(s, vb[k], m_ref, acc_ref, 2 * p + k)

    def attend_two(u, carry):
        attend(2 * u)
        attend(2 * u + 1)
        return carry

    lax.fori_loop(0, n_need // 2, attend_two, 0)
    pl.when(n_need % 2 == 1)(lambda: attend(n_need - 1))
    for p in range(WIDTH // LANES):
        o_ref[:, p * LANES:(p + 1) * LANES] = _pair_output(acc_ref, p).astype(o_ref.dtype)


def _dsa_branch(ua, ub, bsz, seq):
    n = ub.shape[0]
    tq, tk = min(DSA_TQ, seq), min(DSA_TK, seq)
    nq, nk = seq // tq, seq // tk
    assert seq & (seq - 1) == 0 and DSA_TOPK <= seq // 4
    once = pl.Buffered(1)
    return pl.pallas_call(
        functools.partial(_dsa_kernel, tk=tk, seq_bits=seq.bit_length() - 1),
        grid=(bsz, nq),
        in_specs=[
            pl.BlockSpec((tq, WIDTH), lambda b, i: (b * nq + i, B_DQ * LANES // WIDTH)),
            pl.BlockSpec((seq, WIDTH), lambda b, i: (b, B_DK * LANES // WIDTH), pipeline_mode=once),
            pl.BlockSpec((seq, WIDTH), lambda b, i: (b, B_DV * LANES // WIDTH), pipeline_mode=once),
            pl.BlockSpec((tq, 2 * LANES), lambda b, i: (b * nq + i, B_IQ // 2)),
            pl.BlockSpec((tq, LANES), lambda b, i: (b * nq + i, A_IW)),
            pl.BlockSpec((seq, LANES), lambda b, i: (b, B_IK), pipeline_mode=once),
        ],
        out_specs=pl.BlockSpec((tq, WIDTH), lambda b, i: (b * nq + i, 0)),
        out_shape=jax.ShapeDtypeStruct((n, WIDTH), BF16),
        scratch_shapes=[pltpu.VMEM((N_HEADS, tq, LANES), F32), pltpu.VMEM((N_HEADS, tq, LANES), F32),
                        pltpu.VMEM((nk + nk % 2, tq, tk), jnp.int32),
                        pltpu.VMEM((tq, tk), F32)],
        compiler_params=_params(("arbitrary",) * 2),
        name="dsa_attention",
    )(ub, ub, ub, ub, ua, ub)


def _merge_kernel(x_ref, gpre_ref, y0_ref, y1_ref, y2_ref, y3_ref, wg_ref, bg_ref, wb_ref, wo_ref,
                  gpost_ref, o_ref):
    x = x_ref[...]
    d = x.shape[1]
    hb = _rms(x, gpre_ref[...]).astype(BF16)
    merged = jnp.zeros(x.shape, F32)
    for n, y_ref in enumerate((y0_ref, y1_ref, y2_ref, y3_ref)):
        logit = _dot(hb, wg_ref[:, n * d:(n + 1) * d]) + bg_ref[:, n * d:(n + 1) * d]
        gate = 1.0 / (1.0 + jnp.exp(-logit))
        merged = merged + gate * _dot(y_ref[...].astype(BF16), wb_ref[n])
    mix = _dot(merged.astype(BF16), wo_ref[...])
    o_ref[...] = x + _rms(mix, gpost_ref[...])


def _merge(x2, g_pre, ys, w_gate, b_gate, w_branch, w_out, g_post):
    n, d = x2.shape
    tm = MLP_TILE
    const2 = lambda i: (0, 0)
    row = lambda i: (i, 0)
    once = pl.Buffered(1)
    return pl.pallas_call(
        _merge_kernel,
        grid=(n // tm,),
        in_specs=[pl.BlockSpec((tm, d), row), pl.BlockSpec((1, d), const2)]
        + [pl.BlockSpec((tm, WIDTH), row)] * N_BRANCH
        + [pl.BlockSpec((d, N_BRANCH * d), const2, pipeline_mode=once),
           pl.BlockSpec((1, N_BRANCH * d), const2),
           pl.BlockSpec((N_BRANCH, WIDTH, d), lambda i: (0, 0, 0), pipeline_mode=once),
           pl.BlockSpec((d, d), const2, pipeline_mode=once),
           pl.BlockSpec((1, d), const2)],
        out_specs=pl.BlockSpec((tm, d), row),
        out_shape=jax.ShapeDtypeStruct((n, d), F32),
        compiler_params=_params(("arbitrary",)),
        name="branch_merge",
    )(x2, g_pre.reshape(1, d), *ys, w_gate.astype(BF16), b_gate.reshape(1, -1),
      w_branch.astype(BF16), w_out.astype(BF16), g_post.reshape(1, d))


FFN_CHUNK = 256


def _ffn_kernel(x_ref, gpre_ref, wi_ref, wo_ref, gpost_ref, o_ref):
    x = x_ref[...]
    hb = _rms(x, gpre_ref[...]).astype(BF16)
    acc = jnp.zeros(x.shape, F32)
    for c in range(FFN_HIDDEN // FFN_CHUNK):
        lo = c * FFN_CHUNK
        gate = _dot(hb, wi_ref[:, lo:lo + FFN_CHUNK])
        up = _dot(hb, wi_ref[:, FFN_HIDDEN + lo:FFN_HIDDEN + lo + FFN_CHUNK])
        acc = acc + _dot((_silu(gate) * up).astype(BF16), wo_ref[lo:lo + FFN_CHUNK, :])
    o_ref[...] = x + _rms(acc, gpost_ref[...])


def _ffn(x2, g_pre, w_ffn_in, w_ffn_out, g_post):
    n, d = x2.shape
    tm = MLP_TILE
    const2 = lambda i: (0, 0)
    row = lambda i: (i, 0)
    once = pl.Buffered(1)
    return pl.pallas_call(
        _ffn_kernel,
        grid=(n // tm,),
        in_specs=[pl.BlockSpec((tm, d), row), pl.BlockSpec((1, d), const2),
                  pl.BlockSpec((d, 2 * FFN_HIDDEN), const2, pipeline_mode=once),
                  pl.BlockSpec((FFN_HIDDEN, d), const2, pipeline_mode=once),
                  pl.BlockSpec((1, d), const2)],
        out_specs=pl.BlockSpec((tm, d), row),
        out_shape=jax.ShapeDtypeStruct((n, d), F32),
        compiler_params=_params(("arbitrary",)),
        name="swiglu_ffn",
    )(x2, g_pre.reshape(1, d), w_ffn_in.astype(BF16), w_ffn_out.astype(BF16), g_post.reshape(1, d))


def kernel(x, norm_mix_pre, w_in, conv_w, conv_b, dt_bias, a_log, d_skip, ssm_norm, fox_fbias, w_gate, b_gate, w_branch, w_out, norm_mix_post, norm_ffn_pre, w_ffn_in, w_ffn_out, norm_ffn_post):
    bsz, seq, d = x.shape
    x2 = x.reshape(bsz * seq, d)
    cos, sin = _rope_tables(seq)
    for l in range(w_in.shape[0]):
        w, wt = _rearranged_w_in(w_in[l])
        ua, ub, t, km = _in_projection(x2, norm_mix_pre[l], w, wt, cos, sin, fox_fbias[l], seq)
        ys = (_ssd_branch(ua, t, conv_w[l], conv_b[l], dt_bias[l], a_log[l], d_skip[l], ssm_norm[l], bsz, seq),
              _fox_branch(ub, t, bsz, seq),
              _moba_branch(ub, km, bsz, seq),
              _dsa_branch(ua, ub, bsz, seq))
        x2 = _merge(x2, norm_mix_pre[l], ys, w_gate[l], b_gate[l], w_branch[l], w_out[l], norm_mix_post[l])
        x2 = _ffn(x2, norm_ffn_pre[l], w_ffn_in[l], w_ffn_out[l], norm_ffn_post[l])
    return x2.reshape(bsz, seq, d)
```
